```python
import math
import jax
import jax.numpy as jnp
from jax import lax
import numpy as np

D_MODEL = 1024
BATCH = 16
SEQ = 2048
DEPTH = 2

HEAD_DIM = 64
BLOCK = 128
EPS = 1e-6

DIL_PAIRS = ((128, 1), (512, 4), (2048, 16))
DIL_SLOTS = 6
A_HEADS = 18
A_WIDTH = A_HEADS * HEAD_DIM
A_OUT = DIL_SLOTS * HEAD_DIM

SSD_INNER = 2 * D_MODEL
SSD_HEADDIM = 64
SSD_HEADS = SSD_INNER // SSD_HEADDIM
SSD_GROUPS = 8
SSD_HPG = SSD_HEADS // SSD_GROUPS
SSD_STATE = 128
SSD_CONV = 4
SSD_CHUNK = 128
SSD_CONV_DIM = SSD_INNER + 2 * SSD_GROUPS * SSD_STATE

SB_HEADS = 18
SB_WIDTH = SB_HEADS * HEAD_DIM

N_BRANCHES = 3
IN_SPLITS = (A_WIDTH, A_WIDTH, A_WIDTH, SB_WIDTH, SB_WIDTH, SB_WIDTH,
             SSD_INNER, SSD_CONV_DIM, SSD_HEADS, N_BRANCHES * D_MODEL)
IN_WIDTH = 3 * A_WIDTH + 3 * SB_WIDTH + SSD_INNER + SSD_CONV_DIM + SSD_HEADS + N_BRANCHES * D_MODEL

PEER_HEADS = 8
PEER_NKEYS = 128
PEER_EXPERTS = PEER_NKEYS * PEER_NKEYS
PEER_DKEY = 128
PEER_TOPK = 16
PEER_TOKEN_BLOCK = 128

kernel_name = 'hybrid_dilated_ssd_stickbreak_peer'


def rmsnorm(x, g):
    xf = x.astype(jnp.float32)
    var = jnp.mean(xf * xf, axis=-1, keepdims=True)
    return (xf * lax.rsqrt(var + EPS)).astype(x.dtype) * g


def alibi_slopes(n):
    return jnp.exp2(-8.0 * jnp.arange(1, n + 1, dtype=jnp.float32) / n)


def dilated_window_attention(q, k, v, dilation, steps, slopes):
    b, s, h, dh = q.shape
    length = s // dilation
    nb = -(-length // BLOCK)
    pad_len = nb * BLOCK - length

    def to_blocks(t):
        t = t.reshape(b, length, dilation, h, dh).transpose(0, 2, 1, 3, 4)
        t = t.reshape(b * dilation, length, h, dh)
        t = jnp.pad(t, ((0, 0), (0, pad_len), (0, 0), (0, 0)))
        return t.reshape(b * dilation, nb, BLOCK, h, dh)

    qb, kb, vb = to_blocks(q), to_blocks(k), to_blocks(v)

    def with_previous(t):
        prev = jnp.pad(t[:, :-1], ((0, 0), (1, 0), (0, 0), (0, 0), (0, 0)))
        return jnp.concatenate([prev, t], axis=2)

    kb, vb = with_previous(kb), with_previous(vb)
    scores = jnp.einsum('znqhd,znkhd->znhqk', qb, kb).astype(jnp.float32) * (dh ** -0.5)
    qpos = BLOCK + jnp.arange(BLOCK)
    kpos = jnp.arange(2 * BLOCK)
    dist = qpos[:, None] - kpos[None, :]
    local = (dist >= 0) & (dist <= steps)
    first = (jnp.arange(nb) == 0)[:, None, None] & (kpos < BLOCK)[None, None, :]
    valid = local[None] & ~first
    bias = -slopes[:, None, None] * (dist * dilation).astype(jnp.float32)[None]
    scores = jnp.where(valid[None, :, None], scores + bias[None, None], -jnp.inf)
    m = jnp.max(scores, axis=-1, keepdims=True)
    p = jnp.exp(scores - m)
    den = jnp.sum(p, axis=-1, keepdims=True)
    out = jnp.einsum('znhqk,znkhd->znqhd', (p / den).astype(v.dtype), vb)
    lse = (m + jnp.log(den))[..., 0]
    out = out.reshape(b * dilation, nb * BLOCK, h, dh)[:, :length]
    lse = lse.transpose(0, 1, 3, 2).reshape(b * dilation, nb * BLOCK, h)[:, :length]
    out = out.reshape(b, dilation, length, h, dh).transpose(0, 2, 1, 3, 4).reshape(b, s, h, dh)
    lse = lse.reshape(b, dilation, length, h).transpose(0, 2, 1, 3).reshape(b, s, h)
    return out, lse


def dilated_mixer(q, k, v, slopes):
    b, s = q.shape[:2]
    outs, lses = [], []
    for g, (window, dilation) in enumerate(DIL_PAIRS):
        sl = slice(g * DIL_SLOTS, (g + 1) * DIL_SLOTS)
        o, l = dilated_window_attention(q[:, :, sl], k[:, :, sl], v[:, :, sl],
                                        dilation, window // dilation, slopes[sl])
        outs.append(o)
        lses.append(l)
    alpha = jax.nn.softmax(jnp.stack(lses, axis=0), axis=0)
    y = jnp.sum(alpha[..., None] * jnp.stack(outs, axis=0).astype(jnp.float32), axis=0)
    return y.astype(q.dtype).reshape(b, s, A_OUT)


def causal_depthwise_conv(x, w, bias):
    y = lax.conv_general_dilated(x, w[:, None, :], window_strides=(1,),
                                 padding=((SSD_CONV - 1, 0),),
                                 dimension_numbers=('NWC', 'WIO', 'NWC'),
                                 feature_group_count=x.shape[-1])
    return y + bias


def ssd_chunked(x, dt, a, bm, cm):
    b, s, g, hh, p = x.shape
    n = bm.shape[-1]
    c = s // SSD_CHUNK
    x = x.astype(jnp.float32).reshape(b, c, SSD_CHUNK, g, hh, p)
    dt = dt.reshape(b, c, SSD_CHUNK, g, hh)
    bm = bm.astype(jnp.float32).reshape(b, c, SSD_CHUNK, g, n)
    cm = cm.astype(jnp.float32).reshape(b, c, SSD_CHUNK, g, n)
    cum = jnp.cumsum(dt * a, axis=2)
    xdt = x * dt[..., None]
    cum_t = cum.transpose(0, 1, 3, 4, 2)
    seg = cum_t[..., :, None] - cum_t[..., None, :]
    tril = jnp.tril(jnp.ones((SSD_CHUNK, SSD_CHUNK), dtype=bool))
    decay = jnp.exp(jnp.where(tril, seg, -jnp.inf))
    cb = jnp.einsum('bctgn,bcsgn->bcgts', cm, bm)
    y_diag = jnp.einsum('bcghts,bcsghp->bctghp', cb[:, :, :, None] * decay, xdt)
    decay_states = jnp.exp(cum[:, :, -1:] - cum)
    states = jnp.einsum('bclgn,bclgh,bclghp->bcghpn', bm, decay_states, xdt)
    chunk_decay = jnp.exp(cum[:, :, -1])

    def step(h_state, inp):
        st, dec = inp
        return h_state * dec[..., None, None] + st, h_state

    h0 = jnp.zeros((b, g, hh, p, n), jnp.float32)
    _, prev = lax.scan(step, h0, (states.transpose(1, 0, 2, 3, 4, 5),
                                  chunk_decay.transpose(1, 0, 2, 3)))
    prev = prev.transpose(1, 0, 2, 3, 4, 5)
    y_off = jnp.einsum('bctgn,bcghpn,bctgh->bctghp', cm, prev, jnp.exp(cum))
    return (y_diag + y_off).reshape(b, s, g, hh, p)


def ssd_mixer(z, xbc, dt_raw, conv_w, conv_b, dt_bias, a_log, d_skip, norm_g):
    b, s = z.shape[:2]
    xbc = jax.nn.silu(causal_depthwise_conv(xbc, conv_w, conv_b))
    xs = xbc[..., :SSD_INNER].reshape(b, s, SSD_GROUPS, SSD_HPG, SSD_HEADDIM)
    bm = xbc[..., SSD_INNER:SSD_INNER + SSD_GROUPS * SSD_STATE].reshape(b, s, SSD_GROUPS, SSD_STATE)
    cm = xbc[..., SSD_INNER + SSD_GROUPS * SSD_STATE:].reshape(b, s, SSD_GROUPS, SSD_STATE)
    dt = jax.nn.softplus(dt_raw.astype(jnp.float32) + dt_bias.astype(jnp.float32))
    dt = dt.reshape(b, s, SSD_GROUPS, SSD_HPG)
    a = -jnp.exp(a_log.astype(jnp.float32)).reshape(SSD_GROUPS, SSD_HPG)
    y = ssd_chunked(xs, dt, a, bm, cm)
    y = y + d_skip.astype(jnp.float32).reshape(SSD_GROUPS, SSD_HPG)[:, :, None] * xs.astype(jnp.float32)
    y = y.astype(z.dtype).reshape(b, s, SSD_INNER)
    return rmsnorm(y * jax.nn.silu(z), norm_g)


def stick_breaking_attention(q, k, v):
    b, s, h, dh = q.shape
    scale = dh ** -0.5
    outs = []
    for i in range(s // BLOCK):
        kv_len = (i + 1) * BLOCK
        z = jnp.einsum('bqhd,bkhd->bhqk', q[:, i * BLOCK:kv_len], k[:, :kv_len]).astype(jnp.float32) * scale
        tpos = i * BLOCK + jnp.arange(BLOCK)
        causal = jnp.arange(kv_len)[None, :] < tpos[:, None]
        log_beta = jax.nn.log_sigmoid(z)
        log_keep = jnp.where(causal, jax.nn.log_sigmoid(-z), 0.0)
        between = lax.cumsum(log_keep, axis=3, reverse=True) - log_keep
        w = jnp.where(causal, jnp.exp(log_beta + between), 0.0)
        outs.append(jnp.einsum('bhqk,bkhd->bqhd', w.astype(v.dtype), v[:, :kv_len]))
    return jnp.concatenate(outs, axis=1)


def peer_ffn(h, w_query, sub_keys, expert_down, expert_up):
    b, s, d = h.shape
    t = b * s
    hf = h.reshape(t, d)
    q = (hf @ w_query).reshape(t, PEER_HEADS, 2, PEER_DKEY)
    scores = jnp.einsum('thcd,hckd->thck', q, sub_keys).astype(jnp.float32)
    top_s, top_i = lax.top_k(scores, PEER_TOPK)
    cand_s = (top_s[:, :, 0, :, None] + top_s[:, :, 1, None, :]).reshape(t, PEER_HEADS, PEER_TOPK * PEER_TOPK)
    cand_i = (top_i[:, :, 0, :, None] * PEER_NKEYS + top_i[:, :, 1, None, :]).reshape(t, PEER_HEADS, PEER_TOPK * PEER_TOPK)
    best_s, best_pos = lax.top_k(cand_s, PEER_TOPK)
    expert_idx = jnp.take_along_axis(cand_i, best_pos, axis=-1)
    gates = jax.nn.softmax(best_s, axis=-1).astype(h.dtype)
    nblk = t // PEER_TOKEN_BLOCK

    def block_fn(args):
        hb, idx, g = args
        act = jnp.einsum('td,thkd->thk', hb, expert_down[idx])
        return jnp.einsum('thk,thkd->td', g * jax.nn.gelu(act, approximate=False), expert_up[idx])

    y = lax.map(block_fn, (hf.reshape(nblk, PEER_TOKEN_BLOCK, d),
                           expert_idx.reshape(nblk, PEER_TOKEN_BLOCK, PEER_HEADS, PEER_TOPK),
                           gates.reshape(nblk, PEER_TOKEN_BLOCK, PEER_HEADS, PEER_TOPK)))
    return y.reshape(b, s, d)


def setup_inputs(seed: int = 0) -> dict:
    key = jax.random.key(seed)
    ks = jax.random.split(key, 20)

    def nrm(k, shape, scale):
        return jax.random.normal(k, shape, jnp.float32) * scale

    dt0 = jnp.exp(jax.random.uniform(ks[5], (DEPTH, SSD_HEADS), jnp.float32)
                  * (math.log(0.1) - math.log(0.001)) + math.log(0.001))
    return {
        'x': nrm(ks[0], (BATCH, SEQ, D_MODEL), 1.0),
        'norm_mix_g': 1.0 + nrm(ks[1], (DEPTH, D_MODEL), 0.02),
        'w_in': nrm(ks[2], (DEPTH, D_MODEL, IN_WIDTH), D_MODEL ** -0.5),
        'conv_w': nrm(ks[3], (DEPTH, SSD_CONV, SSD_CONV_DIM), SSD_CONV ** -0.5),
        'conv_b': nrm(ks[4], (DEPTH, SSD_CONV_DIM), 0.02),
        'dt_bias': dt0 + jnp.log(-jnp.expm1(-dt0)),
        'a_log': jnp.log(jax.random.uniform(ks[6], (DEPTH, SSD_HEADS), jnp.float32, 1.0, 16.0)),
        'd_skip': 1.0 + nrm(ks[7], (DEPTH, SSD_HEADS), 0.02),
        'ssd_norm_g': 1.0 + nrm(ks[8], (DEPTH, SSD_INNER), 0.02),
        'w_branch_a': nrm(ks[9], (DEPTH, A_OUT, D_MODEL), A_OUT ** -0.5),
        'w_branch_b': nrm(ks[10], (DEPTH, SSD_INNER, D_MODEL), SSD_INNER ** -0.5),
        'w_branch_c': nrm(ks[11], (DEPTH, SB_WIDTH, D_MODEL), SB_WIDTH ** -0.5),
        'w_out': nrm(ks[12], (DEPTH, D_MODEL, D_MODEL), D_MODEL ** -0.5),
        'norm_ffn_g': 1.0 + nrm(ks[13], (DEPTH, D_MODEL), 0.02),
        'peer_w_query': nrm(ks[14], (DEPTH, D_MODEL, PEER_HEADS * 2 * PEER_DKEY), D_MODEL ** -0.5),
        'peer_sub_keys': nrm(ks[15], (DEPTH, PEER_HEADS, 2, PEER_NKEYS, PEER_DKEY), PEER_DKEY ** -0.5),
        'peer_down': nrm(ks[16], (DEPTH, PEER_EXPERTS, D_MODEL), D_MODEL ** -0.5),
        'peer_up': nrm(ks[17], (DEPTH, PEER_EXPERTS, D_MODEL), PEER_HEADS ** -0.5),
        'final_norm_g': 1.0 + nrm(ks[18], (D_MODEL,), 0.02),
    }


def reference(x, norm_mix_g, w_in, conv_w, conv_b, dt_bias, a_log, d_skip, ssd_norm_g,
              w_branch_a, w_branch_b, w_branch_c, w_out, norm_ffn_g,
              peer_w_query, peer_sub_keys, peer_down, peer_up, final_norm_g):
    b, s, d = x.shape
    slopes = alibi_slopes(A_HEADS)
    offsets = []
    acc = 0
    for w in IN_SPLITS[:-1]:
        acc += w
        offsets.append(acc)
    for l in range(DEPTH):
        h = rmsnorm(x, norm_mix_g[l])
        proj = h @ w_in[l]
        qa, ka, va, qc, kc, vc, z, xbc, dt_raw, gate_logits = jnp.split(proj, offsets, axis=-1)
        heads_a = lambda t: t.reshape(b, s, A_HEADS, HEAD_DIM)
        heads_c = lambda t: t.reshape(b, s, SB_HEADS, HEAD_DIM)
        y_a = dilated_mixer(heads_a(qa), heads_a(ka), heads_a(va), slopes)
        y_b = ssd_mixer(z, xbc, dt_raw, conv_w[l], conv_b[l], dt_bias[l], a_log[l],
                        d_skip[l], ssd_norm_g[l])
        y_c = stick_breaking_attention(heads_c(qc), heads_c(kc), heads_c(vc)).reshape(b, s, SB_WIDTH)
        gates = jax.nn.sigmoid(gate_logits.astype(jnp.float32)).astype(x.dtype).reshape(b, s, N_BRANCHES, d)
        merged = (gates[:, :, 0] * (y_a @ w_branch_a[l])
                  + gates[:, :, 1] * (y_b @ w_branch_b[l])
                  + gates[:, :, 2] * (y_c @ w_branch_c[l]))
        x = x + merged @ w_out[l]
        h = rmsnorm(x, norm_ffn_g[l])
        x = x + peer_ffn(h, peer_w_query[l], peer_sub_keys[l], peer_down[l], peer_up[l])
    return rmsnorm(x, final_norm_g)
```

```python
import functools
import math

import jax
import jax.numpy as jnp
from jax import lax
from jax.experimental import pallas as pl
from jax.experimental.pallas import tpu as pltpu

F32 = jnp.float32
BF16 = jnp.bfloat16

EPS = 1e-6
D_MODEL = 1024
HEAD_DIM = 64
BLOCK = 128
LANES = 128
VMEM_LIMIT = 56 * 1024 * 1024

DIL_PAIRS = ((128, 1), (512, 4), (2048, 16))
DIL_SLOTS = 6
A_HEADS = 18
A_WIDTH = A_HEADS * HEAD_DIM
A_OUT = DIL_SLOTS * HEAD_DIM
SB_HEADS = 18
SB_WIDTH = SB_HEADS * HEAD_DIM
SSD_INNER = 2048
SSD_HEADS = 32
SSD_GROUPS = 8
SSD_HPG = 4
SSD_STATE = 128
SSD_CONV = 4
SSD_GW = SSD_HPG * HEAD_DIM
CONV_HALO = 8
PEER_HEADS = 8
PEER_NKEYS = 128
PEER_DKEY = 128
PEER_TOPK = 16
PEER_EXPERTS = PEER_NKEYS * PEER_NKEYS
PEER_IBLK = 8

NEG_INF = float("-inf")


def _params(*sem):
    return pltpu.CompilerParams(dimension_semantics=sem, vmem_limit_bytes=VMEM_LIMIT)


def _dot(a, b):
    return jnp.dot(a, b, preferred_element_type=F32)


def _dot_nt(a, b):
    return lax.dot_general(a, b, (((1,), (1,)), ((), ())), preferred_element_type=F32)


def _split_bf16(v):
    hi = v.astype(BF16)
    lo = (v - hi.astype(F32)).astype(BF16)
    return hi, lo


def _iota2(shape, dim):
    return lax.broadcasted_iota(jnp.int32, shape, dim)


def _norm_matmul_kernel(x_ref, g_ref, w_ref, o_ref, h_ref):
    @pl.when(pl.program_id(1) == 0)
    def _():
        x = x_ref[...]
        var = jnp.mean(x * x, axis=-1, keepdims=True)
        h_ref[...] = (x * lax.rsqrt(var + EPS) * g_ref[...]).astype(BF16)

    o_ref[...] = _dot(h_ref[...], w_ref[...]).astype(o_ref.dtype)


def _norm_matmul(x2d, g, w, out_dtype, tm, tn, name):
    t, k = x2d.shape
    n = w.shape[1]
    return pl.pallas_call(
        _norm_matmul_kernel,
        grid=(t // tm, n // tn),
        in_specs=[pl.BlockSpec((tm, k), lambda i, j: (i, 0)),
                  pl.BlockSpec((1, k), lambda i, j: (0, 0)),
                  pl.BlockSpec((k, tn), lambda i, j: (0, j))],
        out_specs=pl.BlockSpec((tm, tn), lambda i, j: (i, j)),
        out_shape=jax.ShapeDtypeStruct((t, n), out_dtype),
        scratch_shapes=[pltpu.VMEM((tm, k), BF16)],
        compiler_params=_params("arbitrary", "arbitrary"),
        name=name,
    )(x2d, g.reshape(1, k), w)


def _dil_kernel(*refs, slopes, dilation, has_prev):
    if has_prev:
        q_ref, k_ref, v_ref, kp_ref, vp_ref, o_ref, l_ref = refs
    else:
        q_ref, k_ref, v_ref, o_ref, l_ref = refs
    scale = HEAD_DIM ** -0.5
    row = _iota2((BLOCK, BLOCK), 0)
    col = _iota2((BLOCK, BLOCK), 1)
    dist_cur = row - col
    cur_valid = dist_cur >= 0
    dcur_f = (dist_cur * dilation).astype(F32)
    if has_prev:
        dist_prev = dist_cur + BLOCK
        prev_valid = (dist_prev <= BLOCK) & (pl.program_id(2) > 0)
        dprev_f = (dist_prev * dilation).astype(F32)
    outs, lses = [], []
    for h in range(DIL_SLOTS):
        hs = slice(h * HEAD_DIM, (h + 1) * HEAD_DIM)
        qh = q_ref[:, hs]
        s_c = _dot_nt(qh, k_ref[:, hs]) * scale - slopes[h] * dcur_f
        s_c = jnp.where(cur_valid, s_c, NEG_INF)
        m = jnp.max(s_c, axis=-1, keepdims=True)
        if has_prev:
            s_p = _dot_nt(qh, kp_ref[:, hs]) * scale - slopes[h] * dprev_f
            s_p = jnp.where(prev_valid, s_p, NEG_INF)
            m = jnp.maximum(m, jnp.max(s_p, axis=-1, keepdims=True))
        p_c = jnp.exp(s_c - m)
        den = jnp.sum(p_c, axis=-1, keepdims=True)
        if has_prev:
            p_p = jnp.exp(s_p - m)
            den = den + jnp.sum(p_p, axis=-1, keepdims=True)
        inv = 1.0 / den
        o = _dot((p_c * inv).astype(BF16), v_ref[:, hs])
        if has_prev:
            o = o + _dot((p_p * inv).astype(BF16), vp_ref[:, hs])
        outs.append(o)
        lses.append(jnp.broadcast_to(m + jnp.log(den), (BLOCK, HEAD_DIM)))
    o_ref[...] = jnp.concatenate(outs, axis=1)
    l_ref[...] = jnp.concatenate(lses, axis=1)


def _dil_attn(qkv, group, dilation, batch, seq):
    length = seq // dilation
    nb = length // BLOCK
    has_prev = nb > 1
    ncol = 3 * A_WIDTH // A_OUT
    view = qkv.reshape(batch, length, dilation * 3 * A_WIDTH)
    blk = (None, BLOCK, A_OUT)
    nq = A_WIDTH // A_OUT
    q_spec = pl.BlockSpec(blk, lambda b, r, i: (b, i, r * ncol + group))
    k_spec = pl.BlockSpec(blk, lambda b, r, i: (b, i, r * ncol + nq + group))
    v_spec = pl.BlockSpec(blk, lambda b, r, i: (b, i, r * ncol + 2 * nq + group))
    in_specs = [q_spec, k_spec, v_spec]
    args = [view, view, view]
    if has_prev:
        in_specs += [pl.BlockSpec(blk, lambda b, r, i: (b, jnp.maximum(i - 1, 0), r * ncol + nq + group)),
                     pl.BlockSpec(blk, lambda b, r, i: (b, jnp.maximum(i - 1, 0), r * ncol + 2 * nq + group))]
        args += [view, view]
    slopes = tuple(2.0 ** (-8.0 * (group * DIL_SLOTS + j + 1) / A_HEADS) for j in range(DIL_SLOTS))
    out_spec = pl.BlockSpec(blk, lambda b, r, i: (b, i, r))
    shape = jax.ShapeDtypeStruct((batch, length, dilation * A_OUT), F32)
    o, l = pl.pallas_call(
        functools.partial(_dil_kernel, slopes=slopes, dilation=dilation, has_prev=has_prev),
        grid=(batch, dilation, nb),
        in_specs=in_specs,
        out_specs=[out_spec, out_spec],
        out_shape=[shape, shape],
        compiler_params=_params("arbitrary", "arbitrary", "arbitrary"),
        name=f"dil_attn_d{dilation}",
    )(*args)
    return o.reshape(batch * seq, A_OUT), l.reshape(batch * seq, A_OUT)


def _sb_kernel(q_ref, k_ref, v_ref, o_ref, *, seq):
    scale = HEAD_DIM ** -0.5
    row = _iota2((BLOCK, BLOCK), 0)
    col = _iota2((BLOCK, BLOCK), 1)
    later = (row > col).astype(BF16)
    strict = col < row
    nq = seq // BLOCK

    for h in range(LANES // HEAD_DIM):
        hs = slice(h * HEAD_DIM, (h + 1) * HEAD_DIM)

        def block(q, j, carry, acc, diag):
            rows = pl.ds(pl.multiple_of(j * BLOCK, BLOCK), BLOCK)
            z = _dot_nt(q, k_ref[rows, hs]) * scale
            log_beta = jnp.minimum(z, 0.0) - jnp.log(1.0 + jnp.exp(-jnp.abs(z)))
            log_keep = log_beta - z
            if diag:
                log_keep = jnp.where(strict, log_keep, 0.0)
            hi, lo = _split_bf16(log_keep)
            between = _dot(hi, later) + _dot(lo, later)
            w = jnp.exp(log_beta + between + carry)
            if diag:
                w = jnp.where(strict, w, 0.0)
            acc = acc + _dot(w.astype(BF16), v_ref[rows, hs])
            carry = carry + jnp.sum(log_keep, axis=-1, keepdims=True)
            return carry, acc

        def qblock(i, _):
            qrows = pl.ds(pl.multiple_of(i * BLOCK, BLOCK), BLOCK)
            q = q_ref[qrows, hs]
            carry, acc = block(q, i, jnp.zeros((BLOCK, 1), F32), jnp.zeros((BLOCK, HEAD_DIM), F32), True)

            def body(jj, ca):
                return block(q, i - 1 - jj, ca[0], ca[1], False)

            carry, acc = lax.fori_loop(0, i, body, (carry, acc))
            o_ref[qrows, hs] = acc.astype(o_ref.dtype)
            return 0

        lax.fori_loop(0, nq, qblock, 0)


def _sb_attn(qkv, batch, seq):
    view = qkv.reshape(batch, seq, 3 * SB_WIDTH)
    npair = SB_WIDTH // LANES
    blk = (None, seq, LANES)
    out = pl.pallas_call(
        functools.partial(_sb_kernel, seq=seq),
        grid=(batch, npair),
        in_specs=[pl.BlockSpec(blk, lambda b, p: (b, 0, p)),
                  pl.BlockSpec(blk, lambda b, p: (b, 0, npair + p)),
                  pl.BlockSpec(blk, lambda b, p: (b, 0, 2 * npair + p))],
        out_specs=pl.BlockSpec(blk, lambda b, p: (b, 0, p)),
        out_shape=jax.ShapeDtypeStruct((batch, seq, SB_WIDTH), BF16),
        compiler_params=_params("arbitrary", "arbitrary"),
        name="sb_attn",
    )(view, view, view)
    return out.reshape(batch * seq, SB_WIDTH)


def _lane_expand(v):
    rows = v.shape[0]
    return jnp.concatenate([jnp.broadcast_to(v[:, h:h + 1], (rows, HEAD_DIM)) for h in range(SSD_HPG)], axis=1)


def _ssd_kernel(xs_ref, bm_ref, cm_ref, dt_ref, cwx_ref, cwb_ref, cwc_ref, cbx_ref, cbb_ref, cbc_ref,
                dtb_ref, alog_ref, dsk_ref, y_ref, state_ref, *, seq):
    row = _iota2((BLOCK, BLOCK), 0)
    col = _iota2((BLOCK, BLOCK), 1)
    tril = col <= row
    tril_bf = tril.astype(BF16)
    after = row > col
    a_neg = -jnp.exp(alog_ref[...])
    dsk4 = _lane_expand(dsk_ref[...])
    state_ref[...] = jnp.zeros_like(state_ref)

    def conv_silu(ref, w_ref, b_ref, c):
        start = pl.multiple_of(c * BLOCK, BLOCK)
        cur = ref[pl.ds(start, BLOCK), :]
        pstart = pl.multiple_of(jnp.maximum(start - CONV_HALO, 0), CONV_HALO)
        prev = ref[pl.ds(pstart, CONV_HALO), :]
        prev = jnp.where(c > 0, prev, 0.0)
        cat = jnp.concatenate([prev, cur], axis=0)
        w = w_ref[...]
        y = b_ref[...]
        for k in range(SSD_CONV):
            off = CONV_HALO - (SSD_CONV - 1) + k
            y = y + cat[off:off + BLOCK] * w[k:k + 1]
        return y * jax.nn.sigmoid(y)

    def chunk(c, _):
        rows = pl.ds(pl.multiple_of(c * BLOCK, BLOCK), BLOCK)
        xs = conv_silu(xs_ref, cwx_ref, cbx_ref, c)
        bm = conv_silu(bm_ref, cwb_ref, cbb_ref, c)
        cm = conv_silu(cm_ref, cwc_ref, cbc_ref, c)
        dt_in = dt_ref[rows, :] + dtb_ref[...]
        dt = jnp.maximum(dt_in, 0.0) + jnp.log1p(jnp.exp(-jnp.abs(dt_in)))
        dt4 = _lane_expand(dt)
        dta4 = _lane_expand(dt * a_neg)
        hi, lo = _split_bf16(dta4)
        cum4 = _dot(tril_bf, hi) + _dot(tril_bf, lo)
        last = cum4[BLOCK - 1:BLOCK, :]
        xdt = xs * dt4
        bmb = bm.astype(BF16)
        cmb = cm.astype(BF16)
        cb = _dot_nt(cmb, bmb)
        state = state_ref[...]
        y = _dot(cmb, state.astype(BF16)) * jnp.exp(cum4) + dsk4 * xs
        diag = []
        for h in range(SSD_HPG):
            hs = slice(h * HEAD_DIM, (h + 1) * HEAD_DIM)
            dcol = dta4[:, hs]
            a2 = jnp.where(after, jnp.concatenate([dcol, dcol], axis=1), 0.0)
            hi, lo = _split_bf16(a2)
            seg = _dot(tril_bf, hi) + _dot(tril_bf, lo)
            decay = jnp.exp(jnp.where(tril, seg, NEG_INF))
            diag.append(_dot((cb * decay).astype(BF16), xdt[:, hs].astype(BF16)))
        y_ref[rows, :] = y + jnp.concatenate(diag, axis=1)
        new = _dot(bm.T.astype(BF16), (jnp.exp(last - cum4) * xdt).astype(BF16))
        state_ref[...] = state * jnp.exp(last) + new
        return 0

    lax.fori_loop(0, seq // BLOCK, chunk, 0)


def _ssd(zx, dt_raw, conv_w, conv_b, dt_bias, a_log, d_skip, batch, seq):
    width = zx.shape[1]
    view = zx.reshape(batch, seq, width)
    x0 = SSD_INNER // SSD_GW
    b0 = 2 * SSD_INNER // SSD_STATE
    c0 = b0 + SSD_GROUPS
    dt_t = dt_raw.reshape(batch, seq, SSD_GROUPS, SSD_HPG).transpose(0, 2, 1, 3)
    cb2 = conv_b.reshape(1, -1)
    per_head = lambda v: v.reshape(SSD_GROUPS, 1, SSD_HPG)
    small = pl.BlockSpec((None, 1, SSD_HPG), lambda b, g: (g, 0, 0))
    out = pl.pallas_call(
        functools.partial(_ssd_kernel, seq=seq),
        grid=(batch, SSD_GROUPS),
        in_specs=[pl.BlockSpec((None, seq, SSD_GW), lambda b, g: (b, 0, x0 + g)),
                  pl.BlockSpec((None, seq, SSD_STATE), lambda b, g: (b, 0, b0 + g)),
                  pl.BlockSpec((None, seq, SSD_STATE), lambda b, g: (b, 0, c0 + g)),
                  pl.BlockSpec((None, None, seq, SSD_HPG), lambda b, g: (b, g, 0, 0)),
                  pl.BlockSpec((SSD_CONV, SSD_GW), lambda b, g: (0, g)),
                  pl.BlockSpec((SSD_CONV, SSD_STATE), lambda b, g: (0, b0 - 2 * SSD_GROUPS + g)),
                  pl.BlockSpec((SSD_CONV, SSD_STATE), lambda b, g: (0, c0 - 2 * SSD_GROUPS + g)),
                  pl.BlockSpec((1, SSD_GW), lambda b, g: (0, g)),
                  pl.BlockSpec((1, SSD_STATE), lambda b, g: (0, b0 - 2 * SSD_GROUPS + g)),
                  pl.BlockSpec((1, SSD_STATE), lambda b, g: (0, c0 - 2 * SSD_GROUPS + g)),
                  small, small, small],
        out_specs=pl.BlockSpec((None, seq, SSD_GW), lambda b, g: (b, 0, g)),
        out_shape=jax.ShapeDtypeStruct((batch, seq, SSD_INNER), F32),
        scratch_shapes=[pltpu.VMEM((SSD_STATE, SSD_GW), F32)],
        compiler_params=_params("arbitrary", "arbitrary"),
        name="ssd",
    )(view, view, view, dt_t, conv_w, conv_w, conv_w, cb2, cb2, cb2,
      per_head(dt_bias), per_head(a_log), per_head(d_skip))
    return out.reshape(batch * seq, SSD_INNER)


def _merge_kernel(o1_ref, o2_ref, o3_ref, l1_ref, l2_ref, l3_ref, ys_ref, z_ref, yc_ref, gl_ref, x_ref,
                  wa_ref, wb_ref, wc_ref, wo_ref, ng_ref, fg_ref, xo_ref, ho_ref):
    l1, l2, l3 = l1_ref[...], l2_ref[...], l3_ref[...]
    m = jnp.maximum(jnp.maximum(l1, l2), l3)
    e1, e2, e3 = jnp.exp(l1 - m), jnp.exp(l2 - m), jnp.exp(l3 - m)
    inv = 1.0 / (e1 + e2 + e3)
    y_a = (e1 * inv) * o1_ref[...] + (e2 * inv) * o2_ref[...] + (e3 * inv) * o3_ref[...]
    z = z_ref[...]
    yb = ys_ref[...] * (z * jax.nn.sigmoid(z))
    var = jnp.mean(yb * yb, axis=-1, keepdims=True)
    yb = yb * lax.rsqrt(var + EPS) * ng_ref[...]
    gates = jax.nn.sigmoid(gl_ref[...])
    merged = (gates[:, :D_MODEL] * _dot(y_a.astype(BF16), wa_ref[...])
              + gates[:, D_MODEL:2 * D_MODEL] * _dot(yb.astype(BF16), wb_ref[...])
              + gates[:, 2 * D_MODEL:] * _dot(yc_ref[...], wc_ref[...]))
    x_new = x_ref[...] + _dot(merged.astype(BF16), wo_ref[...])
    xo_ref[...] = x_new
    var = jnp.mean(x_new * x_new, axis=-1, keepdims=True)
    ho_ref[...] = (x_new * lax.rsqrt(var + EPS) * fg_ref[...]).astype(BF16)


def _merge_out(outs, lses, y_ssd, zx, y_c, gate_logits, x2d, wa, wb, wc, wo, ssd_norm_g, norm_ffn_g, tm):
    t = x2d.shape[0]
    row = lambda w: pl.BlockSpec((tm, w), lambda i: (i, 0))
    full = lambda a: pl.BlockSpec(a.shape, lambda i: (0, 0))
    ng = ssd_norm_g.reshape(1, -1)
    fg = norm_ffn_g.reshape(1, -1)
    return pl.pallas_call(
        _merge_kernel,
        grid=(t // tm,),
        in_specs=[row(A_OUT)] * 6 + [row(SSD_INNER), row(SSD_INNER), row(SB_WIDTH), row(3 * D_MODEL), row(D_MODEL),
                                     full(wa), full(wb), full(wc), full(wo), full(ng), full(fg)],
        out_specs=[row(D_MODEL), row(D_MODEL)],
        out_shape=[jax.ShapeDtypeStruct((t, D_MODEL), F32), jax.ShapeDtypeStruct((t, D_MODEL), BF16)],
        compiler_params=_params("arbitrary"),
        name="merge_out",
    )(*outs, *lses, y_ssd, zx, y_c, gate_logits, x2d, wa, wb, wc, wo, ng, fg)


def _top_rows(s, k):
    vals = []
    for _ in range(k):
        m = jnp.max(s, axis=0, keepdims=True)
        vals.append(m)
        s = jnp.where(s == m, NEG_INF, s)
    return jnp.concatenate(vals, axis=0)


def _kth_pair_sum(a, b):
    half = PEER_TOPK // 2
    parts = [a[0:1] + b]
    parts += [a[i:i + 1] + b[:half] for i in range(1, half)]
    parts += [a[half:] + b[0:1]]
    cand = jnp.concatenate(parts, axis=0)
    seen = jnp.zeros_like(a[0:1])
    tau = jnp.full_like(a[0:1], NEG_INF)
    for _ in range(PEER_TOPK):
        m = jnp.max(cand, axis=0, keepdims=True)
        hit = cand == m
        seen = seen + jnp.sum(hit.astype(F32), axis=0, keepdims=True)
        tau = jnp.maximum(tau, jnp.where(seen >= PEER_TOPK, m, NEG_INF))
        cand = jnp.where(hit, NEG_INF, cand)
    return tau


def _peer_kernel(h_ref, x_ref, wq_ref, sk_ref, down_ref, upt_ref, o_ref,
                 s1_ref, s2_ref, e1_ref, e2_ref, tau_ref, act_ref, g_ref, acc_ref):
    eb = pl.program_id(1)
    hb = h_ref[...]

    @pl.when(eb == 0)
    def _prep():
        q = _dot(hb, wq_ref[...]).astype(BF16)
        for h in range(PEER_HEADS):
            for c, ref in enumerate((s1_ref, s2_ref)):
                off = (2 * h + c) * PEER_DKEY
                ref[h] = _dot_nt(sk_ref[h, c], q[:, off:off + PEER_DKEY])

        def head(h, _):
            s1 = s1_ref[h]
            s2 = s2_ref[h]
            a = _top_rows(s1, PEER_TOPK)
            b = _top_rows(s2, PEER_TOPK)
            tau = _kth_pair_sum(a, b)
            a0, b0 = a[0:1], b[0:1]
            e1 = jnp.exp(s1 - a0)
            e2 = jnp.exp(s2 - b0)
            za = jnp.exp(a - a0)
            zb = jnp.exp(b - b0)
            zsum = jnp.zeros_like(tau)
            for i in range(PEER_TOPK):
                sel = (a[i:i + 1] + b) >= tau
                zsum = zsum + za[i:i + 1] * jnp.sum(jnp.where(sel, zb, 0.0), axis=0, keepdims=True)
            e1_ref[h] = e1 * (1.0 / zsum)
            e2_ref[h] = e2
            tau_ref[h] = tau
            return 0

        lax.fori_loop(0, PEER_HEADS, head, 0)
        acc_ref[...] = jnp.zeros_like(acc_ref)

    act_ref[...] = _dot_nt(down_ref[...], hb)

    def iblock(ii, _):
        i = eb * PEER_IBLK + ii
        w = jnp.zeros((PEER_NKEYS, hb.shape[0]), F32)
        for h in range(PEER_HEADS):
            s1row = s1_ref[h, pl.ds(i, 1), :]
            e1row = e1_ref[h, pl.ds(i, 1), :]
            sel = (s1row + s2_ref[h]) >= tau_ref[h]
            w = w + jnp.where(sel, e2_ref[h] * e1row, 0.0)
        rows = pl.ds(pl.multiple_of(ii * PEER_NKEYS, PEER_NKEYS), PEER_NKEYS)
        a = act_ref[rows, :]
        gelu = 0.5 * a * (1.0 + lax.erf(a * (2.0 ** -0.5)))
        g_ref[rows, :] = (w * gelu).astype(BF16)
        return 0

    lax.fori_loop(0, PEER_IBLK, iblock, 0)
    acc_ref[...] += _dot(upt_ref[...], g_ref[...])

    @pl.when(eb == pl.num_programs(1) - 1)
    def _():
        o_ref[...] = x_ref[...] + acc_ref[...].T


def _peer(h2, x2d, wq, sub_keys, down, up_t, tm):
    t = x2d.shape[0]
    eblk = PEER_IBLK * PEER_NKEYS
    nheads_keys = (PEER_HEADS, PEER_NKEYS, tm)
    return pl.pallas_call(
        _peer_kernel,
        grid=(t // tm, PEER_EXPERTS // eblk),
        in_specs=[pl.BlockSpec((tm, D_MODEL), lambda i, e: (i, 0)),
                  pl.BlockSpec((tm, D_MODEL), lambda i, e: (i, 0)),
                  pl.BlockSpec(wq.shape, lambda i, e: (0, 0)),
                  pl.BlockSpec(sub_keys.shape, lambda i, e: (0, 0, 0, 0)),
                  pl.BlockSpec((eblk, D_MODEL), lambda i, e: (e, 0)),
                  pl.BlockSpec((D_MODEL, eblk), lambda i, e: (0, e))],
        out_specs=pl.BlockSpec((tm, D_MODEL), lambda i, e: (i, 0)),
        out_shape=jax.ShapeDtypeStruct((t, D_MODEL), F32),
        scratch_shapes=[pltpu.VMEM(nheads_keys, F32), pltpu.VMEM(nheads_keys, F32),
                        pltpu.VMEM(nheads_keys, F32), pltpu.VMEM(nheads_keys, F32),
                        pltpu.VMEM((PEER_HEADS, 1, tm), F32),
                        pltpu.VMEM((eblk, tm), F32), pltpu.VMEM((eblk, tm), BF16),
                        pltpu.VMEM((D_MODEL, tm), F32)],
        compiler_params=_params("arbitrary", "arbitrary"),
        name="peer",
    )(h2, x2d, wq, sub_keys, down, up_t)


def _rmsnorm_kernel(x_ref, g_ref, o_ref):
    x = x_ref[...]
    var = jnp.mean(x * x, axis=-1, keepdims=True)
    o_ref[...] = x * lax.rsqrt(var + EPS) * g_ref[...]


def _rmsnorm(x2d, g, tm):
    t, d = x2d.shape
    return pl.pallas_call(
        _rmsnorm_kernel,
        grid=(t // tm,),
        in_specs=[pl.BlockSpec((tm, d), lambda i: (i, 0)), pl.BlockSpec((1, d), lambda i: (0, 0))],
        out_specs=pl.BlockSpec((tm, d), lambda i: (i, 0)),
        out_shape=jax.ShapeDtypeStruct((t, d), F32),
        compiler_params=_params("arbitrary"),
        name="final_norm",
    )(x2d, g.reshape(1, d))


def kernel(x, norm_mix_g, w_in, conv_w, conv_b, dt_bias, a_log, d_skip, ssd_norm_g, w_branch_a, w_branch_b,
           w_branch_c, w_out, norm_ffn_g, peer_w_query, peer_sub_keys, peer_down, peer_up, final_norm_g):
    batch, seq, d = x.shape
    x2d = x.reshape(batch * seq, d)
    depth = w_in.shape[0]
    o_c = 3 * A_WIDTH
    o_zx = o_c + 3 * SB_WIDTH
    o_dt = o_zx + 3 * SSD_INNER
    o_g = o_dt + SSD_HEADS
    for l in range(depth):
        w = w_in[l]
        g = norm_mix_g[l]
        qkv_a = _norm_matmul(x2d, g, w[:, :o_c].astype(BF16), BF16, 1024, A_WIDTH, "proj_qkv_a")
        qkv_c = _norm_matmul(x2d, g, w[:, o_c:o_zx].astype(BF16), BF16, 1024, SB_WIDTH, "proj_qkv_c")
        zx = _norm_matmul(x2d, g, w[:, o_zx:o_dt].astype(BF16), F32, 1024, 1536, "proj_zx")
        dt_raw = _norm_matmul(x2d, g, w[:, o_dt:o_g].astype(BF16), F32, 1024, SSD_HEADS, "proj_dt")
        gate_logits = _norm_matmul(x2d, g, w[:, o_g:].astype(BF16), F32, 1024, 1536, "proj_gates")

        outs, lses = [], []
        for grp, (_, dilation) in enumerate(DIL_PAIRS):
            o, lse = _dil_attn(qkv_a, grp, dilation, batch, seq)
            outs.append(o)
            lses.append(lse)
        y_ssd = _ssd(zx, dt_raw, conv_w[l], conv_b[l], dt_bias[l], a_log[l], d_skip[l], batch, seq)
        y_c = _sb_attn(qkv_c, batch, seq)

        x2d, h2 = _merge_out(outs, lses, y_ssd, zx, y_c, gate_logits, x2d,
                             w_branch_a[l].astype(BF16), w_branch_b[l].astype(BF16), w_branch_c[l].astype(BF16),
                             w_out[l].astype(BF16), ssd_norm_g[l], norm_ffn_g[l], 256)
        x2d = _peer(h2, x2d, peer_w_query[l].astype(BF16), peer_sub_keys[l].astype(BF16),
                    peer_down[l].astype(BF16), peer_up[l].T.astype(BF16), 256)
    return _rmsnorm(x2d, final_norm_g, 1024).reshape(batch, seq, d)
```

```python
import functools
import math

import jax
import jax.numpy as jnp
from jax import lax
from jax.experimental import pallas as pl
from jax.experimental.pallas import tpu as pltpu

F32 = jnp.float32
BF16 = jnp.bfloat16

EPS = 1e-6
D_MODEL = 1024
HEAD_DIM = 64
BLOCK = 128
LANES = 128
VMEM_LIMIT = 56 * 1024 * 1024

DIL_PAIRS = ((128, 1), (512, 4), (2048, 16))
DIL_SLOTS = 6
A_HEADS = 18
A_WIDTH = A_HEADS * HEAD_DIM
A_OUT = DIL_SLOTS * HEAD_DIM
SB_HEADS = 18
SB_WIDTH = SB_HEADS * HEAD_DIM
SB_TQ = 512
SSD_INNER = 2048
SSD_HEADS = 32
SSD_GROUPS = 8
SSD_HPG = 4
SSD_STATE = 128
SSD_CONV = 4
SSD_GW = SSD_HPG * HEAD_DIM
CONV_HALO = 8
PEER_HEADS = 8
PEER_NKEYS = 128
PEER_DKEY = 128
PEER_TOPK = 16
PEER_EXPERTS = PEER_NKEYS * PEER_NKEYS
PEER_IBLK = 8

NEG_INF = float("-inf")


def _params(*sem):
    return pltpu.CompilerParams(dimension_semantics=sem, vmem_limit_bytes=VMEM_LIMIT)


def _dot(a, b):
    return jnp.dot(a, b, preferred_element_type=F32)


def _dot_nt(a, b):
    return lax.dot_general(a, b, (((1,), (1,)), ((), ())), preferred_element_type=F32)


def _split_bf16(v):
    hi = v.astype(BF16)
    lo = (v - hi.astype(F32)).astype(BF16)
    return hi, lo


def _iota2(shape, dim):
    return lax.broadcasted_iota(jnp.int32, shape, dim)


def _norm_matmul_kernel(x_ref, g_ref, w_ref, o_ref, h_ref):
    @pl.when(pl.program_id(1) == 0)
    def _():
        x = x_ref[...]
        var = jnp.mean(x * x, axis=-1, keepdims=True)
        h_ref[...] = (x * lax.rsqrt(var + EPS) * g_ref[...]).astype(BF16)

    o_ref[...] = _dot(h_ref[...], w_ref[...]).astype(o_ref.dtype)


def _norm_matmul(x2d, g, w, out_dtype, tm, tn, name):
    t, k = x2d.shape
    n = w.shape[1]
    return pl.pallas_call(
        _norm_matmul_kernel,
        grid=(t // tm, n // tn),
        in_specs=[pl.BlockSpec((tm, k), lambda i, j: (i, 0)),
                  pl.BlockSpec((1, k), lambda i, j: (0, 0)),
                  pl.BlockSpec((k, tn), lambda i, j: (0, j))],
        out_specs=pl.BlockSpec((tm, tn), lambda i, j: (i, j)),
        out_shape=jax.ShapeDtypeStruct((t, n), out_dtype),
        scratch_shapes=[pltpu.VMEM((tm, k), BF16)],
        compiler_params=_params("arbitrary", "arbitrary"),
        name=name,
    )(x2d, g.reshape(1, k), w)


def _dil_kernel(*refs, slopes, dilation, has_prev):
    if has_prev:
        q_ref, k_ref, v_ref, kp_ref, vp_ref, o_ref, l_ref = refs
    else:
        q_ref, k_ref, v_ref, o_ref, l_ref = refs
    scale = HEAD_DIM ** -0.5
    row = _iota2((BLOCK, BLOCK), 0)
    col = _iota2((BLOCK, BLOCK), 1)
    dist_cur = row - col
    cur_valid = dist_cur >= 0
    dcur_f = (dist_cur * dilation).astype(F32)
    if has_prev:
        dist_prev = dist_cur + BLOCK
        prev_valid = (dist_prev <= BLOCK) & (pl.program_id(2) > 0)
        dprev_f = (dist_prev * dilation).astype(F32)
    outs, lses = [], []
    for h in range(DIL_SLOTS):
        hs = slice(h * HEAD_DIM, (h + 1) * HEAD_DIM)
        qh = q_ref[:, hs]
        s_c = _dot_nt(qh, k_ref[:, hs]) * scale - slopes[h] * dcur_f
        s_c = jnp.where(cur_valid, s_c, NEG_INF)
        m = jnp.max(s_c, axis=-1, keepdims=True)
        if has_prev:
            s_p = _dot_nt(qh, kp_ref[:, hs]) * scale - slopes[h] * dprev_f
            s_p = jnp.where(prev_valid, s_p, NEG_INF)
            m = jnp.maximum(m, jnp.max(s_p, axis=-1, keepdims=True))
        p_c = jnp.exp(s_c - m)
        den = jnp.sum(p_c, axis=-1, keepdims=True)
        if has_prev:
            p_p = jnp.exp(s_p - m)
            den = den + jnp.sum(p_p, axis=-1, keepdims=True)
        inv = 1.0 / den
        o = _dot((p_c * inv).astype(BF16), v_ref[:, hs])
        if has_prev:
            o = o + _dot((p_p * inv).astype(BF16), vp_ref[:, hs])
        outs.append(o)
        lses.append(jnp.broadcast_to(m + jnp.log(den), (BLOCK, HEAD_DIM)))
    o_ref[...] = jnp.concatenate(outs, axis=1)
    l_ref[...] = jnp.concatenate(lses, axis=1)


def _dil_attn(qkv, group, dilation, batch, seq):
    length = seq // dilation
    nb = length // BLOCK
    has_prev = nb > 1
    ncol = 3 * A_WIDTH // A_OUT
    view = qkv.reshape(batch, length, dilation * 3 * A_WIDTH)
    blk = (None, BLOCK, A_OUT)
    nq = A_WIDTH // A_OUT
    q_spec = pl.BlockSpec(blk, lambda b, r, i: (b, i, r * ncol + group))
    k_spec = pl.BlockSpec(blk, lambda b, r, i: (b, i, r * ncol + nq + group))
    v_spec = pl.BlockSpec(blk, lambda b, r, i: (b, i, r * ncol + 2 * nq + group))
    in_specs = [q_spec, k_spec, v_spec]
    args = [view, view, view]
    if has_prev:
        in_specs += [pl.BlockSpec(blk, lambda b, r, i: (b, jnp.maximum(i - 1, 0), r * ncol + nq + group)),
                     pl.BlockSpec(blk, lambda b, r, i: (b, jnp.maximum(i - 1, 0), r * ncol + 2 * nq + group))]
        args += [view, view]
    slopes = tuple(2.0 ** (-8.0 * (group * DIL_SLOTS + j + 1) / A_HEADS) for j in range(DIL_SLOTS))
    out_spec = pl.BlockSpec(blk, lambda b, r, i: (b, i, r))
    shape = jax.ShapeDtypeStruct((batch, length, dilation * A_OUT), F32)
    o, l = pl.pallas_call(
        functools.partial(_dil_kernel, slopes=slopes, dilation=dilation, has_prev=has_prev),
        grid=(batch, dilation, nb),
        in_specs=in_specs,
        out_specs=[out_spec, out_spec],
        out_shape=[shape, shape],
        compiler_params=_params("arbitrary", "arbitrary", "arbitrary"),
        name=f"dil_attn_d{dilation}",
    )(*args)
    return o.reshape(batch * seq, A_OUT), l.reshape(batch * seq, A_OUT)


def _head_block_diag(m):
    mf = m.astype(F32)
    first = _iota2(mf.shape, 1) < HEAD_DIM
    return jnp.concatenate([jnp.where(first, mf, 0.0), jnp.where(first, 0.0, mf)], axis=0).astype(BF16)


def _sb_kernel(q_ref, k_ref, v_ref, o_ref, carry_ref, acc_ref, *, seq):
    scale = HEAD_DIM ** -0.5
    nsub = SB_TQ // BLOCK
    nheads = LANES // HEAD_DIM
    r2 = _iota2((2 * BLOCK, 2 * BLOCK), 0) & (BLOCK - 1)
    c2 = _iota2((2 * BLOCK, 2 * BLOCK), 1)
    cum_w = jnp.where((r2 > c2) | (c2 >= BLOCK), 1.0, 0.0).astype(BF16)
    strict_full = _iota2((SB_TQ, BLOCK), 1) < _iota2((SB_TQ, BLOCK), 0)

    def block(i, j, r0, masked):
        n = SB_TQ - r0
        q = q_ref[pl.ds(pl.multiple_of(i * SB_TQ + r0, BLOCK), n), :]
        krows = pl.ds(pl.multiple_of(j * BLOCK, BLOCK), BLOCK)
        z2 = _dot_nt(q, _head_block_diag(k_ref[krows, :])) * scale
        strict = strict_full[:n]
        ws = []
        for h in range(nheads):
            z = z2[:, h * BLOCK:(h + 1) * BLOCK]
            log_beta = jnp.minimum(z, 0.0) - jnp.log(1.0 + jnp.exp(-jnp.abs(z)))
            log_keep = log_beta - z
            if masked:
                log_keep = jnp.where(strict, log_keep, 0.0)
            hi, lo = _split_bf16(log_keep)
            cum = _dot(jnp.concatenate([hi, lo], axis=1), cum_w)
            carry = carry_ref[h, r0:, :]
            w = jnp.exp(log_beta + cum[:, :BLOCK] + carry)
            if masked:
                w = jnp.where(strict, w, 0.0)
            ws.append(w.astype(BF16))
            carry_ref[h, r0:, :] = carry + cum[:, BLOCK:]
        acc_ref[r0:, :] += _dot(jnp.concatenate(ws, axis=1), _head_block_diag(v_ref[krows, :]))

    def qtile(i, _):
        carry_ref[...] = jnp.zeros_like(carry_ref)
        acc_ref[...] = jnp.zeros_like(acc_ref)
        for jj in reversed(range(nsub)):
            block(i, i * nsub + jj, jj * BLOCK, True)

        def body(t, _):
            block(i, i * nsub - 1 - t, 0, False)
            return 0

        lax.fori_loop(0, i * nsub, body, 0)
        o_ref[pl.ds(pl.multiple_of(i * SB_TQ, SB_TQ), SB_TQ), :] = acc_ref[...].astype(o_ref.dtype)
        return 0

    lax.fori_loop(0, seq // SB_TQ, qtile, 0)


def _sb_attn(qkv, batch, seq):
    view = qkv.reshape(batch, seq, 3 * SB_WIDTH)
    npair = SB_WIDTH // LANES
    blk = (None, seq, LANES)
    out = pl.pallas_call(
        functools.partial(_sb_kernel, seq=seq),
        grid=(batch, npair),
        in_specs=[pl.BlockSpec(blk, lambda b, p: (b, 0, p)),
                  pl.BlockSpec(blk, lambda b, p: (b, 0, npair + p)),
                  pl.BlockSpec(blk, lambda b, p: (b, 0, 2 * npair + p))],
        out_specs=pl.BlockSpec(blk, lambda b, p: (b, 0, p)),
        out_shape=jax.ShapeDtypeStruct((batch, seq, SB_WIDTH), BF16),
        scratch_shapes=[pltpu.VMEM((LANES // HEAD_DIM, SB_TQ, BLOCK), F32), pltpu.VMEM((SB_TQ, LANES), F32)],
        compiler_params=_params("arbitrary", "arbitrary"),
        name="sb_attn",
    )(view, view, view)
    return out.reshape(batch * seq, SB_WIDTH)


def _lane_expand(v):
    rows = v.shape[0]
    return jnp.concatenate([jnp.broadcast_to(v[:, h:h + 1], (rows, HEAD_DIM)) for h in range(SSD_HPG)], axis=1)


def _ssd_kernel(xs_ref, bm_ref, cm_ref, dt_ref, cwx_ref, cwb_ref, cwc_ref, cbx_ref, cbb_ref, cbc_ref,
                dtb_ref, alog_ref, dsk_ref, y_ref, state_ref, *, seq):
    row = _iota2((BLOCK, BLOCK), 0)
    col = _iota2((BLOCK, BLOCK), 1)
    tril = col <= row
    tril_bf = tril.astype(BF16)
    after = row > col
    a_neg = -jnp.exp(alog_ref[...])
    dsk4 = _lane_expand(dsk_ref[...])
    state_ref[...] = jnp.zeros_like(state_ref)

    def conv_silu(ref, w_ref, b_ref, c):
        start = pl.multiple_of(c * BLOCK, BLOCK)
        cur = ref[pl.ds(start, BLOCK), :]
        pstart = pl.multiple_of(jnp.maximum(start - CONV_HALO, 0), CONV_HALO)
        prev = ref[pl.ds(pstart, CONV_HALO), :]
        prev = jnp.where(c > 0, prev, 0.0)
        cat = jnp.concatenate([prev, cur], axis=0)
        w = w_ref[...]
        y = b_ref[...]
        for k in range(SSD_CONV):
            off = CONV_HALO - (SSD_CONV - 1) + k
            y = y + cat[off:off + BLOCK] * w[k:k + 1]
        return y * jax.nn.sigmoid(y)

    def chunk(c, _):
        rows = pl.ds(pl.multiple_of(c * BLOCK, BLOCK), BLOCK)
        xs = conv_silu(xs_ref, cwx_ref, cbx_ref, c)
        bm = conv_silu(bm_ref, cwb_ref, cbb_ref, c)
        cm = conv_silu(cm_ref, cwc_ref, cbc_ref, c)
        dt_in = dt_ref[rows, :] + dtb_ref[...]
        dt = jnp.maximum(dt_in, 0.0) + jnp.log1p(jnp.exp(-jnp.abs(dt_in)))
        dt4 = _lane_expand(dt)
        dta4 = _lane_expand(dt * a_neg)
        hi, lo = _split_bf16(dta4)
        cum4 = _dot(tril_bf, hi) + _dot(tril_bf, lo)
        last = cum4[BLOCK - 1:BLOCK, :]
        xdt = xs * dt4
        bmb = bm.astype(BF16)
        cmb = cm.astype(BF16)
        cb = _dot_nt(cmb, bmb)
        state = state_ref[...]
        y = _dot(cmb, state.astype(BF16)) * jnp.exp(cum4) + dsk4 * xs
        diag = []
        for h in range(SSD_HPG):
            hs = slice(h * HEAD_DIM, (h + 1) * HEAD_DIM)
            dcol = dta4[:, hs]
            a2 = jnp.where(after, jnp.concatenate([dcol, dcol], axis=1), 0.0)
            hi, lo = _split_bf16(a2)
            seg = _dot(tril_bf, hi) + _dot(tril_bf, lo)
            decay = jnp.exp(jnp.where(tril, seg, NEG_INF))
            diag.append(_dot((cb * decay).astype(BF16), xdt[:, hs].astype(BF16)))
        y_ref[rows, :] = y + jnp.concatenate(diag, axis=1)
        new = _dot(bm.T.astype(BF16), (jnp.exp(last - cum4) * xdt).astype(BF16))
        state_ref[...] = state * jnp.exp(last) + new
        return 0

    lax.fori_loop(0, seq // BLOCK, chunk, 0)


def _ssd(zx, dt_raw, conv_w, conv_b, dt_bias, a_log, d_skip, batch, seq):
    width = zx.shape[1]
    view = zx.reshape(batch, seq, width)
    x0 = SSD_INNER // SSD_GW
    b0 = 2 * SSD_INNER // SSD_STATE
    c0 = b0 + SSD_GROUPS
    dt_t = dt_raw.reshape(batch, seq, SSD_GROUPS, SSD_HPG).transpose(0, 2, 1, 3)
    cb2 = conv_b.reshape(1, -1)
    per_head = lambda v: v.reshape(SSD_GROUPS, 1, SSD_HPG)
    small = pl.BlockSpec((None, 1, SSD_HPG), lambda b, g: (g, 0, 0))
    out = pl.pallas_call(
        functools.partial(_ssd_kernel, seq=seq),
        grid=(batch, SSD_GROUPS),
        in_specs=[pl.BlockSpec((None, seq, SSD_GW), lambda b, g: (b, 0, x0 + g)),
                  pl.BlockSpec((None, seq, SSD_STATE), lambda b, g: (b, 0, b0 + g)),
                  pl.BlockSpec((None, seq, SSD_STATE), lambda b, g: (b, 0, c0 + g)),
                  pl.BlockSpec((None, None, seq, SSD_HPG), lambda b, g: (b, g, 0, 0)),
                  pl.BlockSpec((SSD_CONV, SSD_GW), lambda b, g: (0, g)),
                  pl.BlockSpec((SSD_CONV, SSD_STATE), lambda b, g: (0, b0 - 2 * SSD_GROUPS + g)),
                  pl.BlockSpec((SSD_CONV, SSD_STATE), lambda b, g: (0, c0 - 2 * SSD_GROUPS + g)),
                  pl.BlockSpec((1, SSD_GW), lambda b, g: (0, g)),
                  pl.BlockSpec((1, SSD_STATE), lambda b, g: (0, b0 - 2 * SSD_GROUPS + g)),
                  pl.BlockSpec((1, SSD_STATE), lambda b, g: (0, c0 - 2 * SSD_GROUPS + g)),
                  small, small, small],
        out_specs=pl.BlockSpec((None, seq, SSD_GW), lambda b, g: (b, 0, g)),
        out_shape=jax.ShapeDtypeStruct((batch, seq, SSD_INNER), F32),
        scratch_shapes=[pltpu.VMEM((SSD_STATE, SSD_GW), F32)],
        compiler_params=_params("arbitrary", "arbitrary"),
        name="ssd",
    )(view, view, view, dt_t, conv_w, conv_w, conv_w, cb2, cb2, cb2,
      per_head(dt_bias), per_head(a_log), per_head(d_skip))
    return out.reshape(batch * seq, SSD_INNER)


def _merge_kernel(o1_ref, o2_ref, o3_ref, l1_ref, l2_ref, l3_ref, ys_ref, z_ref, yc_ref, gl_ref, x_ref,
                  wa_ref, wb_ref, wc_ref, wo_ref, ng_ref, fg_ref, xo_ref, ho_ref):
    l1, l2, l3 = l1_ref[...], l2_ref[...], l3_ref[...]
    m = jnp.maximum(jnp.maximum(l1, l2), l3)
    e1, e2, e3 = jnp.exp(l1 - m), jnp.exp(l2 - m), jnp.exp(l3 - m)
    inv = 1.0 / (e1 + e2 + e3)
    y_a = (e1 * inv) * o1_ref[...] + (e2 * inv) * o2_ref[...] + (e3 * inv) * o3_ref[...]
    z = z_ref[...]
    yb = ys_ref[...] * (z * jax.nn.sigmoid(z))
    var = jnp.mean(yb * yb, axis=-1, keepdims=True)
    yb = yb * lax.rsqrt(var + EPS) * ng_ref[...]
    gates = jax.nn.sigmoid(gl_ref[...])
    merged = (gates[:, :D_MODEL] * _dot(y_a.astype(BF16), wa_ref[...])
              + gates[:, D_MODEL:2 * D_MODEL] * _dot(yb.astype(BF16), wb_ref[...])
              + gates[:, 2 * D_MODEL:] * _dot(yc_ref[...], wc_ref[...]))
    x_new = x_ref[...] + _dot(merged.astype(BF16), wo_ref[...])
    xo_ref[...] = x_new
    var = jnp.mean(x_new * x_new, axis=-1, keepdims=True)
    ho_ref[...] = (x_new * lax.rsqrt(var + EPS) * fg_ref[...]).astype(BF16)


def _merge_out(outs, lses, y_ssd, zx, y_c, gate_logits, x2d, wa, wb, wc, wo, ssd_norm_g, norm_ffn_g, tm):
    t = x2d.shape[0]
    row = lambda w: pl.BlockSpec((tm, w), lambda i: (i, 0))
    full = lambda a: pl.BlockSpec(a.shape, lambda i: (0, 0))
    ng = ssd_norm_g.reshape(1, -1)
    fg = norm_ffn_g.reshape(1, -1)
    return pl.pallas_call(
        _merge_kernel,
        grid=(t // tm,),
        in_specs=[row(A_OUT)] * 6 + [row(SSD_INNER), row(SSD_INNER), row(SB_WIDTH), row(3 * D_MODEL), row(D_MODEL),
                                     full(wa), full(wb), full(wc), full(wo), full(ng), full(fg)],
        out_specs=[row(D_MODEL), row(D_MODEL)],
        out_shape=[jax.ShapeDtypeStruct((t, D_MODEL), F32), jax.ShapeDtypeStruct((t, D_MODEL), BF16)],
        compiler_params=_params("arbitrary"),
        name="merge_out",
    )(*outs, *lses, y_ssd, zx, y_c, gate_logits, x2d, wa, wb, wc, wo, ng, fg)


def _top_rows(s, k):
    vals = []
    for _ in range(k):
        m = jnp.max(s, axis=0, keepdims=True)
        vals.append(m)
        s = jnp.where(s == m, NEG_INF, s)
    return jnp.concatenate(vals, axis=0)


def _kth_pair_sum(a, b):
    half = PEER_TOPK // 2
    parts = [a[0:1] + b]
    parts += [a[i:i + 1] + b[:half] for i in range(1, half)]
    parts += [a[half:] + b[0:1]]
    cand = jnp.concatenate(parts, axis=0)
    seen = jnp.zeros_like(a[0:1])
    tau = jnp.full_like(a[0:1], NEG_INF)
    for _ in range(PEER_TOPK):
        m = jnp.max(cand, axis=0, keepdims=True)
        hit = cand == m
        seen = seen + jnp.sum(hit.astype(F32), axis=0, keepdims=True)
        tau = jnp.maximum(tau, jnp.where(seen >= PEER_TOPK, m, NEG_INF))
        cand = jnp.where(hit, NEG_INF, cand)
    return tau


def _peer_kernel(h_ref, x_ref, wq_ref, sk_ref, down_ref, upt_ref, o_ref,
                 s1_ref, s2_ref, e1_ref, e2_ref, tau_ref, g_ref, acc_ref):
    e = pl.program_id(1)
    nblk = pl.num_programs(1) - 1
    hb = h_ref[...]
    tm = hb.shape[0]

    @pl.when(e == 0)
    def _prep():
        q = _dot(hb, wq_ref[...]).astype(BF16)
        for h in range(PEER_HEADS):
            for c, ref in enumerate((s1_ref, s2_ref)):
                off = (2 * h + c) * PEER_DKEY
                ref[h] = _dot_nt(sk_ref[h, c], q[:, off:off + PEER_DKEY])

        def head(h, _):
            s1 = s1_ref[h]
            s2 = s2_ref[h]
            a = _top_rows(s1, PEER_TOPK)
            b = _top_rows(s2, PEER_TOPK)
            tau = _kth_pair_sum(a, b)
            a0, b0 = a[0:1], b[0:1]
            za = jnp.exp(a - a0)
            zb = jnp.exp(b - b0)
            zsum = jnp.zeros_like(tau)
            for i in range(PEER_TOPK):
                sel = (a[i:i + 1] + b) >= tau
                zsum = zsum + za[i:i + 1] * jnp.sum(jnp.where(sel, zb, 0.0), axis=0, keepdims=True)
            e1_ref[h] = jnp.exp(s1 - a0) * (1.0 / zsum)
            e2_ref[h] = jnp.exp(s2 - b0)
            tau_ref[h] = tau
            return 0

        lax.fori_loop(0, PEER_HEADS, head, 0)
        acc_ref[...] = jnp.zeros_like(acc_ref)
        g_ref[...] = jnp.zeros_like(g_ref)

    eb = jnp.minimum(e, nblk - 1)
    act = _dot_nt(down_ref[...], hb)
    acc_ref[...] += _dot(upt_ref[...], g_ref[(e + 1) % 2])
    for ii in range(PEER_IBLK):
        i = eb * PEER_IBLK + ii
        w = jnp.zeros((PEER_NKEYS, tm), F32)
        for h in range(PEER_HEADS):
            s1row = s1_ref[h, pl.ds(i, 1), :]
            e1row = e1_ref[h, pl.ds(i, 1), :]
            sel = (s1row + s2_ref[h]) >= tau_ref[h]
            w = w + jnp.where(sel, e2_ref[h] * e1row, 0.0)
        a = act[ii * PEER_NKEYS:(ii + 1) * PEER_NKEYS]
        gelu = 0.5 * a * (1.0 + lax.erf(a * (2.0 ** -0.5)))
        g_ref[e % 2, ii * PEER_NKEYS:(ii + 1) * PEER_NKEYS, :] = (w * gelu).astype(BF16)

    @pl.when(e == nblk)
    def _():
        o_ref[...] = x_ref[...] + acc_ref[...].T


def _peer(h2, x2d, wq, sub_keys, down, up_t, tm):
    t = x2d.shape[0]
    eblk = PEER_IBLK * PEER_NKEYS
    nblk = PEER_EXPERTS // eblk
    nheads_keys = (PEER_HEADS, PEER_NKEYS, tm)
    return pl.pallas_call(
        _peer_kernel,
        grid=(t // tm, nblk + 1),
        in_specs=[pl.BlockSpec((tm, D_MODEL), lambda i, e: (i, 0)),
                  pl.BlockSpec((tm, D_MODEL), lambda i, e: (i, 0)),
                  pl.BlockSpec(wq.shape, lambda i, e: (0, 0)),
                  pl.BlockSpec(sub_keys.shape, lambda i, e: (0, 0, 0, 0)),
                  pl.BlockSpec((eblk, D_MODEL), lambda i, e: (jnp.minimum(e, nblk - 1), 0)),
                  pl.BlockSpec((D_MODEL, eblk), lambda i, e: (0, jnp.maximum(e - 1, 0)))],
        out_specs=pl.BlockSpec((tm, D_MODEL), lambda i, e: (i, 0)),
        out_shape=jax.ShapeDtypeStruct((t, D_MODEL), F32),
        scratch_shapes=[pltpu.VMEM(nheads_keys, F32), pltpu.VMEM(nheads_keys, F32),
                        pltpu.VMEM(nheads_keys, F32), pltpu.VMEM(nheads_keys, F32),
                        pltpu.VMEM((PEER_HEADS, 1, tm), F32),
                        pltpu.VMEM((2, eblk, tm), BF16),
                        pltpu.VMEM((D_MODEL, tm), F32)],
        compiler_params=_params("arbitrary", "arbitrary"),
        name="peer",
    )(h2, x2d, wq, sub_keys, down, up_t)


def _rmsnorm_kernel(x_ref, g_ref, o_ref):
    x = x_ref[...]
    var = jnp.mean(x * x, axis=-1, keepdims=True)
    o_ref[...] = x * lax.rsqrt(var + EPS) * g_ref[...]


def _rmsnorm(x2d, g, tm):
    t, d = x2d.shape
    return pl.pallas_call(
        _rmsnorm_kernel,
        grid=(t // tm,),
        in_specs=[pl.BlockSpec((tm, d), lambda i: (i, 0)), pl.BlockSpec((1, d), lambda i: (0, 0))],
        out_specs=pl.BlockSpec((tm, d), lambda i: (i, 0)),
        out_shape=jax.ShapeDtypeStruct((t, d), F32),
        compiler_params=_params("arbitrary"),
        name="final_norm",
    )(x2d, g.reshape(1, d))


def kernel(x, norm_mix_g, w_in, conv_w, conv_b, dt_bias, a_log, d_skip, ssd_norm_g, w_branch_a, w_branch_b,
           w_branch_c, w_out, norm_ffn_g, peer_w_query, peer_sub_keys, peer_down, peer_up, final_norm_g):
    batch, seq, d = x.shape
    x2d = x.reshape(batch * seq, d)
    depth = w_in.shape[0]
    o_c = 3 * A_WIDTH
    o_zx = o_c + 3 * SB_WIDTH
    o_dt = o_zx + 3 * SSD_INNER
    o_g = o_dt + SSD_HEADS
    for l in range(depth):
        w = w_in[l]
        g = norm_mix_g[l]
        qkv_a = _norm_matmul(x2d, g, w[:, :o_c].astype(BF16), BF16, 1024, A_WIDTH, "proj_qkv_a")
        qkv_c = _norm_matmul(x2d, g, w[:, o_c:o_zx].astype(BF16), BF16, 1024, SB_WIDTH, "proj_qkv_c")
        zx = _norm_matmul(x2d, g, w[:, o_zx:o_dt].astype(BF16), F32, 1024, 1536, "proj_zx")
        dt_raw = _norm_matmul(x2d, g, w[:, o_dt:o_g].astype(BF16), F32, 1024, SSD_HEADS, "proj_dt")
        gate_logits = _norm_matmul(x2d, g, w[:, o_g:].astype(BF16), F32, 1024, 1536, "proj_gates")

        outs, lses = [], []
        for grp, (_, dilation) in enumerate(DIL_PAIRS):
            o, lse = _dil_attn(qkv_a, grp, dilation, batch, seq)
            outs.append(o)
            lses.append(lse)
        y_ssd = _ssd(zx, dt_raw, conv_w[l], conv_b[l], dt_bias[l], a_log[l], d_skip[l], batch, seq)
        y_c = _sb_attn(qkv_c, batch, seq)

        x2d, h2 = _merge_out(outs, lses, y_ssd, zx, y_c, gate_logits, x2d,
                             w_branch_a[l].astype(BF16), w_branch_b[l].astype(BF16), w_branch_c[l].astype(BF16),
                             w_out[l].astype(BF16), ssd_norm_g[l], norm_ffn_g[l], 256)
        x2d = _peer(h2, x2d, peer_w_query[l].astype(BF16), peer_sub_keys[l].astype(BF16),
                    peer_down[l].astype(BF16), peer_up[l].T.astype(BF16), 256)
    return _rmsnorm(x2d, final_norm_g, 1024).reshape(batch, seq, d)
```

```python
import functools
import math

import jax
import jax.numpy as jnp
from jax import lax
from jax.experimental import pallas as pl
from jax.experimental.pallas import tpu as pltpu

F32 = jnp.float32
BF16 = jnp.bfloat16

EPS = 1e-6
D_MODEL = 1024
HEAD_DIM = 64
BLOCK = 128
LANES = 128
VMEM_LIMIT = 56 * 1024 * 1024

DIL_PAIRS = ((128, 1), (512, 4), (2048, 16))
DIL_SLOTS = 6
A_HEADS = 18
A_WIDTH = A_HEADS * HEAD_DIM
A_OUT = DIL_SLOTS * HEAD_DIM
SB_HEADS = 18
SB_WIDTH = SB_HEADS * HEAD_DIM
SB_TQ = 512
SSD_INNER = 2048
SSD_HEADS = 32
SSD_GROUPS = 8
SSD_HPG = 4
SSD_STATE = 128
SSD_CONV = 4
SSD_GW = SSD_HPG * HEAD_DIM
CONV_HALO = 8
PEER_HEADS = 8
PEER_NKEYS = 128
PEER_DKEY = 128
PEER_TOPK = 16
PEER_EXPERTS = PEER_NKEYS * PEER_NKEYS
PEER_IBLK = 8

NEG_INF = float("-inf")


def _params(*sem):
    return pltpu.CompilerParams(dimension_semantics=sem, vmem_limit_bytes=VMEM_LIMIT)


def _dot(a, b):
    return jnp.dot(a, b, preferred_element_type=F32)


def _dot_nt(a, b):
    return lax.dot_general(a, b, (((1,), (1,)), ((), ())), preferred_element_type=F32)


def _split_bf16(v):
    hi = v.astype(BF16)
    lo = (v - hi.astype(F32)).astype(BF16)
    return hi, lo


def _iota2(shape, dim):
    return lax.broadcasted_iota(jnp.int32, shape, dim)


def _norm_matmul_kernel(x_ref, g_ref, w_ref, o_ref, h_ref):
    @pl.when(pl.program_id(1) == 0)
    def _():
        x = x_ref[...]
        var = jnp.mean(x * x, axis=-1, keepdims=True)
        h_ref[...] = (x * lax.rsqrt(var + EPS) * g_ref[...]).astype(BF16)

    o_ref[...] = _dot(h_ref[...], w_ref[...]).astype(o_ref.dtype)


def _norm_matmul(x2d, g, w, out_dtype, tm, tn, name):
    t, k = x2d.shape
    n = w.shape[1]
    return pl.pallas_call(
        _norm_matmul_kernel,
        grid=(t // tm, n // tn),
        in_specs=[pl.BlockSpec((tm, k), lambda i, j: (i, 0)),
                  pl.BlockSpec((1, k), lambda i, j: (0, 0)),
                  pl.BlockSpec((k, tn), lambda i, j: (0, j))],
        out_specs=pl.BlockSpec((tm, tn), lambda i, j: (i, j)),
        out_shape=jax.ShapeDtypeStruct((t, n), out_dtype),
        scratch_shapes=[pltpu.VMEM((tm, k), BF16)],
        compiler_params=_params("arbitrary", "arbitrary"),
        name=name,
    )(x2d, g.reshape(1, k), w)


def _dil_kernel(*refs, slopes, dilation, has_prev):
    if has_prev:
        q_ref, k_ref, v_ref, kp_ref, vp_ref, o_ref, l_ref = refs
    else:
        q_ref, k_ref, v_ref, o_ref, l_ref = refs
    scale = HEAD_DIM ** -0.5
    row = _iota2((BLOCK, BLOCK), 0)
    col = _iota2((BLOCK, BLOCK), 1)
    dist_cur = row - col
    cur_valid = dist_cur >= 0
    dcur_f = (dist_cur * dilation).astype(F32)
    if has_prev:
        dist_prev = dist_cur + BLOCK
        prev_valid = (dist_prev <= BLOCK) & (pl.program_id(2) > 0)
        dprev_f = (dist_prev * dilation).astype(F32)
    outs, lses = [], []
    for h in range(DIL_SLOTS):
        hs = slice(h * HEAD_DIM, (h + 1) * HEAD_DIM)
        qh = q_ref[:, hs]
        s_c = _dot_nt(qh, k_ref[:, hs]) * scale - slopes[h] * dcur_f
        s_c = jnp.where(cur_valid, s_c, NEG_INF)
        m = jnp.max(s_c, axis=-1, keepdims=True)
        if has_prev:
            s_p = _dot_nt(qh, kp_ref[:, hs]) * scale - slopes[h] * dprev_f
            s_p = jnp.where(prev_valid, s_p, NEG_INF)
            m = jnp.maximum(m, jnp.max(s_p, axis=-1, keepdims=True))
        p_c = jnp.exp(s_c - m)
        den = jnp.sum(p_c, axis=-1, keepdims=True)
        if has_prev:
            p_p = jnp.exp(s_p - m)
            den = den + jnp.sum(p_p, axis=-1, keepdims=True)
        inv = 1.0 / den
        o = _dot((p_c * inv).astype(BF16), v_ref[:, hs])
        if has_prev:
            o = o + _dot((p_p * inv).astype(BF16), vp_ref[:, hs])
        outs.append(o)
        lses.append(jnp.broadcast_to(m + jnp.log(den), (BLOCK, HEAD_DIM)))
    o_ref[...] = jnp.concatenate(outs, axis=1)
    l_ref[...] = jnp.concatenate(lses, axis=1)


def _dil_attn(qkv, group, dilation, batch, seq):
    length = seq // dilation
    nb = length // BLOCK
    has_prev = nb > 1
    ncol = 3
    view = qkv.reshape(batch, length, dilation * ncol * A_OUT)
    blk = (None, BLOCK, A_OUT)
    q_spec = pl.BlockSpec(blk, lambda b, r, i: (b, i, r * ncol))
    k_spec = pl.BlockSpec(blk, lambda b, r, i: (b, i, r * ncol + 1))
    v_spec = pl.BlockSpec(blk, lambda b, r, i: (b, i, r * ncol + 2))
    in_specs = [q_spec, k_spec, v_spec]
    args = [view, view, view]
    if has_prev:
        in_specs += [pl.BlockSpec(blk, lambda b, r, i: (b, jnp.maximum(i - 1, 0), r * ncol + 1)),
                     pl.BlockSpec(blk, lambda b, r, i: (b, jnp.maximum(i - 1, 0), r * ncol + 2))]
        args += [view, view]
    slopes = tuple(2.0 ** (-8.0 * (group * DIL_SLOTS + j + 1) / A_HEADS) for j in range(DIL_SLOTS))
    out_spec = pl.BlockSpec(blk, lambda b, r, i: (b, i, r))
    shape = jax.ShapeDtypeStruct((batch, length, dilation * A_OUT), F32)
    o, l = pl.pallas_call(
        functools.partial(_dil_kernel, slopes=slopes, dilation=dilation, has_prev=has_prev),
        grid=(batch, dilation, nb),
        in_specs=in_specs,
        out_specs=[out_spec, out_spec],
        out_shape=[shape, shape],
        compiler_params=_params("arbitrary", "arbitrary", "arbitrary"),
        name=f"dil_attn_d{dilation}",
    )(*args)
    return o.reshape(batch * seq, A_OUT), l.reshape(batch * seq, A_OUT)


def _head_block_diag(m):
    mf = m.astype(F32)
    first = _iota2(mf.shape, 1) < HEAD_DIM
    return jnp.concatenate([jnp.where(first, mf, 0.0), jnp.where(first, 0.0, mf)], axis=0).astype(BF16)


def _sb_kernel(q_ref, k_ref, v_ref, o_ref, carry_ref, acc_ref, *, seq):
    scale = HEAD_DIM ** -0.5
    nsub = SB_TQ // BLOCK
    nheads = LANES // HEAD_DIM
    r2 = _iota2((2 * BLOCK, 2 * BLOCK), 0) & (BLOCK - 1)
    c2 = _iota2((2 * BLOCK, 2 * BLOCK), 1)
    cum_w = jnp.where((r2 > c2) | (c2 >= BLOCK), 1.0, 0.0).astype(BF16)
    strict_full = _iota2((SB_TQ, BLOCK), 1) < _iota2((SB_TQ, BLOCK), 0)

    def block(i, j, r0, masked):
        n = SB_TQ - r0
        q = q_ref[pl.ds(pl.multiple_of(i * SB_TQ + r0, BLOCK), n), :]
        krows = pl.ds(pl.multiple_of(j * BLOCK, BLOCK), BLOCK)
        z2 = _dot_nt(q, _head_block_diag(k_ref[krows, :])) * scale
        strict = strict_full[:n]
        ws = []
        for h in range(nheads):
            z = z2[:, h * BLOCK:(h + 1) * BLOCK]
            log_beta = jnp.minimum(z, 0.0) - jnp.log(1.0 + jnp.exp(-jnp.abs(z)))
            log_keep = log_beta - z
            if masked:
                log_keep = jnp.where(strict, log_keep, 0.0)
            hi, lo = _split_bf16(log_keep)
            cum = _dot(jnp.concatenate([hi, lo], axis=1), cum_w)
            carry = carry_ref[h, r0:, :]
            w = jnp.exp(log_beta + cum[:, :BLOCK] + carry)
            if masked:
                w = jnp.where(strict, w, 0.0)
            ws.append(w.astype(BF16))
            carry_ref[h, r0:, :] = carry + cum[:, BLOCK:]
        acc_ref[r0:, :] += _dot(jnp.concatenate(ws, axis=1), _head_block_diag(v_ref[krows, :]))

    def qtile(i, _):
        carry_ref[...] = jnp.zeros_like(carry_ref)
        acc_ref[...] = jnp.zeros_like(acc_ref)
        for jj in reversed(range(nsub)):
            block(i, i * nsub + jj, jj * BLOCK, True)

        def body(t, _):
            block(i, i * nsub - 1 - 2 * t, 0, False)
            block(i, i * nsub - 2 - 2 * t, 0, False)
            return 0

        lax.fori_loop(0, i * (nsub // 2), body, 0)
        o_ref[pl.ds(pl.multiple_of(i * SB_TQ, SB_TQ), SB_TQ), :] = acc_ref[...].astype(o_ref.dtype)
        return 0

    lax.fori_loop(0, seq // SB_TQ, qtile, 0)


def _sb_attn(qkv, batch, seq):
    view = qkv.reshape(batch, seq, 3 * SB_WIDTH)
    npair = SB_WIDTH // LANES
    blk = (None, seq, LANES)
    out = pl.pallas_call(
        functools.partial(_sb_kernel, seq=seq),
        grid=(batch, npair),
        in_specs=[pl.BlockSpec(blk, lambda b, p: (b, 0, p)),
                  pl.BlockSpec(blk, lambda b, p: (b, 0, npair + p)),
                  pl.BlockSpec(blk, lambda b, p: (b, 0, 2 * npair + p))],
        out_specs=pl.BlockSpec(blk, lambda b, p: (b, 0, p)),
        out_shape=jax.ShapeDtypeStruct((batch, seq, SB_WIDTH), BF16),
        scratch_shapes=[pltpu.VMEM((LANES // HEAD_DIM, SB_TQ, BLOCK), F32), pltpu.VMEM((SB_TQ, LANES), F32)],
        compiler_params=_params("arbitrary", "arbitrary"),
        name="sb_attn",
    )(view, view, view)
    return out.reshape(batch * seq, SB_WIDTH)


def _lane_expand(v):
    rows = v.shape[0]
    return jnp.concatenate([jnp.broadcast_to(v[:, h:h + 1], (rows, HEAD_DIM)) for h in range(SSD_HPG)], axis=1)


def _ssd_kernel(xs_ref, bm_ref, cm_ref, dt_ref, cwx_ref, cwb_ref, cwc_ref, cbx_ref, cbb_ref, cbc_ref,
                dtb_ref, alog_ref, dsk_ref, y_ref, state_ref, *, seq):
    row = _iota2((BLOCK, BLOCK), 0)
    col = _iota2((BLOCK, BLOCK), 1)
    tril = col <= row
    tril_bf = tril.astype(BF16)
    after = row > col
    a_neg = -jnp.exp(alog_ref[...])
    dsk4 = _lane_expand(dsk_ref[...])
    state_ref[...] = jnp.zeros_like(state_ref)

    def conv_silu(ref, w_ref, b_ref, c):
        start = pl.multiple_of(c * BLOCK, BLOCK)
        cur = ref[pl.ds(start, BLOCK), :]
        pstart = pl.multiple_of(jnp.maximum(start - CONV_HALO, 0), CONV_HALO)
        prev = ref[pl.ds(pstart, CONV_HALO), :]
        prev = jnp.where(c > 0, prev, 0.0)
        cat = jnp.concatenate([prev, cur], axis=0)
        w = w_ref[...]
        y = b_ref[...]
        for k in range(SSD_CONV):
            off = CONV_HALO - (SSD_CONV - 1) + k
            y = y + cat[off:off + BLOCK] * w[k:k + 1]
        return y * jax.nn.sigmoid(y)

    def chunk(c, _):
        rows = pl.ds(pl.multiple_of(c * BLOCK, BLOCK), BLOCK)
        xs = conv_silu(xs_ref, cwx_ref, cbx_ref, c)
        bm = conv_silu(bm_ref, cwb_ref, cbb_ref, c)
        cm = conv_silu(cm_ref, cwc_ref, cbc_ref, c)
        dt_in = dt_ref[rows, :] + dtb_ref[...]
        dt = jnp.maximum(dt_in, 0.0) + jnp.log1p(jnp.exp(-jnp.abs(dt_in)))
        dt4 = _lane_expand(dt)
        dta4 = _lane_expand(dt * a_neg)
        hi, lo = _split_bf16(dta4)
        cum4 = _dot(tril_bf, hi) + _dot(tril_bf, lo)
        last = cum4[BLOCK - 1:BLOCK, :]
        xdt = xs * dt4
        bmb = bm.astype(BF16)
        cmb = cm.astype(BF16)
        cb = _dot_nt(cmb, bmb)
        state = state_ref[...]
        y = _dot(cmb, state.astype(BF16)) * jnp.exp(cum4) + dsk4 * xs
        diag = []
        for h in range(SSD_HPG):
            hs = slice(h * HEAD_DIM, (h + 1) * HEAD_DIM)
            dcol = dta4[:, hs]
            a2 = jnp.where(after, jnp.concatenate([dcol, dcol], axis=1), 0.0)
            hi, lo = _split_bf16(a2)
            seg = _dot(tril_bf, hi) + _dot(tril_bf, lo)
            decay = jnp.exp(jnp.where(tril, seg, NEG_INF))
            diag.append(_dot((cb * decay).astype(BF16), xdt[:, hs].astype(BF16)))
        y_ref[rows, :] = y + jnp.concatenate(diag, axis=1)
        new = _dot(bm.T.astype(BF16), (jnp.exp(last - cum4) * xdt).astype(BF16))
        state_ref[...] = state * jnp.exp(last) + new
        return 0

    lax.fori_loop(0, seq // BLOCK, chunk, 0, unroll=2)


def _ssd(zx, dt_raw, conv_w, conv_b, dt_bias, a_log, d_skip, batch, seq):
    width = zx.shape[1]
    view = zx.reshape(batch, seq, width)
    x0 = SSD_INNER // SSD_GW
    b0 = 2 * SSD_INNER // SSD_STATE
    c0 = b0 + SSD_GROUPS
    dt_t = dt_raw.reshape(batch, seq, SSD_GROUPS, SSD_HPG).transpose(0, 2, 1, 3)
    cb2 = conv_b.reshape(1, -1)
    per_head = lambda v: v.reshape(SSD_GROUPS, 1, SSD_HPG)
    small = pl.BlockSpec((None, 1, SSD_HPG), lambda b, g: (g, 0, 0))
    out = pl.pallas_call(
        functools.partial(_ssd_kernel, seq=seq),
        grid=(batch, SSD_GROUPS),
        in_specs=[pl.BlockSpec((None, seq, SSD_GW), lambda b, g: (b, 0, x0 + g)),
                  pl.BlockSpec((None, seq, SSD_STATE), lambda b, g: (b, 0, b0 + g)),
                  pl.BlockSpec((None, seq, SSD_STATE), lambda b, g: (b, 0, c0 + g)),
                  pl.BlockSpec((None, None, seq, SSD_HPG), lambda b, g: (b, g, 0, 0)),
                  pl.BlockSpec((SSD_CONV, SSD_GW), lambda b, g: (0, g)),
                  pl.BlockSpec((SSD_CONV, SSD_STATE), lambda b, g: (0, b0 - 2 * SSD_GROUPS + g)),
                  pl.BlockSpec((SSD_CONV, SSD_STATE), lambda b, g: (0, c0 - 2 * SSD_GROUPS + g)),
                  pl.BlockSpec((1, SSD_GW), lambda b, g: (0, g)),
                  pl.BlockSpec((1, SSD_STATE), lambda b, g: (0, b0 - 2 * SSD_GROUPS + g)),
                  pl.BlockSpec((1, SSD_STATE), lambda b, g: (0, c0 - 2 * SSD_GROUPS + g)),
                  small, small, small],
        out_specs=pl.BlockSpec((None, seq, SSD_GW), lambda b, g: (b, 0, g)),
        out_shape=jax.ShapeDtypeStruct((batch, seq, SSD_INNER), F32),
        scratch_shapes=[pltpu.VMEM((SSD_STATE, SSD_GW), F32)],
        compiler_params=_params("arbitrary", "arbitrary"),
        name="ssd",
    )(view, view, view, dt_t, conv_w, conv_w, conv_w, cb2, cb2, cb2,
      per_head(dt_bias), per_head(a_log), per_head(d_skip))
    return out.reshape(batch * seq, SSD_INNER)


def _merge_kernel(o1_ref, o2_ref, o3_ref, l1_ref, l2_ref, l3_ref, ys_ref, z_ref, yc_ref, gl_ref, x_ref,
                  wa_ref, wb_ref, wc_ref, wo_ref, ng_ref, fg_ref, xo_ref, ho_ref):
    l1, l2, l3 = l1_ref[...], l2_ref[...], l3_ref[...]
    m = jnp.maximum(jnp.maximum(l1, l2), l3)
    e1, e2, e3 = jnp.exp(l1 - m), jnp.exp(l2 - m), jnp.exp(l3 - m)
    inv = 1.0 / (e1 + e2 + e3)
    y_a = (e1 * inv) * o1_ref[...] + (e2 * inv) * o2_ref[...] + (e3 * inv) * o3_ref[...]
    z = z_ref[...]
    yb = ys_ref[...] * (z * jax.nn.sigmoid(z))
    var = jnp.mean(yb * yb, axis=-1, keepdims=True)
    yb = yb * lax.rsqrt(var + EPS) * ng_ref[...]
    gates = jax.nn.sigmoid(gl_ref[...])
    merged = (gates[:, :D_MODEL] * _dot(y_a.astype(BF16), wa_ref[...])
              + gates[:, D_MODEL:2 * D_MODEL] * _dot(yb.astype(BF16), wb_ref[...])
              + gates[:, 2 * D_MODEL:] * _dot(yc_ref[...], wc_ref[...]))
    x_new = x_ref[...] + _dot(merged.astype(BF16), wo_ref[...])
    xo_ref[...] = x_new
    var = jnp.mean(x_new * x_new, axis=-1, keepdims=True)
    ho_ref[...] = (x_new * lax.rsqrt(var + EPS) * fg_ref[...]).astype(BF16)


def _merge_out(outs, lses, y_ssd, zx, y_c, gate_logits, x2d, wa, wb, wc, wo, ssd_norm_g, norm_ffn_g, tm):
    t = x2d.shape[0]
    row = lambda w: pl.BlockSpec((tm, w), lambda i: (i, 0))
    full = lambda a: pl.BlockSpec(a.shape, lambda i: (0, 0))
    ng = ssd_norm_g.reshape(1, -1)
    fg = norm_ffn_g.reshape(1, -1)
    return pl.pallas_call(
        _merge_kernel,
        grid=(t // tm,),
        in_specs=[row(A_OUT)] * 6 + [row(SSD_INNER), row(SSD_INNER), row(SB_WIDTH), row(3 * D_MODEL), row(D_MODEL),
                                     full(wa), full(wb), full(wc), full(wo), full(ng), full(fg)],
        out_specs=[row(D_MODEL), row(D_MODEL)],
        out_shape=[jax.ShapeDtypeStruct((t, D_MODEL), F32), jax.ShapeDtypeStruct((t, D_MODEL), BF16)],
        compiler_params=_params("arbitrary"),
        name="merge_out",
    )(*outs, *lses, y_ssd, zx, y_c, gate_logits, x2d, wa, wb, wc, wo, ng, fg)


def _top_rows(s, k):
    vals = []
    for _ in range(k):
        m = jnp.max(s, axis=0, keepdims=True)
        vals.append(m)
        s = jnp.where(s == m, NEG_INF, s)
    return jnp.concatenate(vals, axis=0)


def _kth_pair_sum(a, b):
    half = PEER_TOPK // 2
    parts = [a[0:1] + b]
    parts += [a[i:i + 1] + b[:half] for i in range(1, half)]
    parts += [a[half:] + b[0:1]]
    cand = jnp.concatenate(parts, axis=0)
    seen = jnp.zeros_like(a[0:1])
    tau = jnp.full_like(a[0:1], NEG_INF)
    for _ in range(PEER_TOPK):
        m = jnp.max(cand, axis=0, keepdims=True)
        hit = cand == m
        seen = seen + jnp.sum(hit.astype(F32), axis=0, keepdims=True)
        tau = jnp.maximum(tau, jnp.where(seen >= PEER_TOPK, m, NEG_INF))
        cand = jnp.where(hit, NEG_INF, cand)
    return tau


def _peer_kernel(h_ref, x_ref, wq_ref, sk_ref, down_ref, upt_ref, o_ref,
                 s1_ref, s2_ref, e1_ref, e2_ref, tau_ref, g_ref, acc_ref):
    e = pl.program_id(1)
    nblk = pl.num_programs(1) - 1
    hb = h_ref[...]
    tm = hb.shape[0]

    @pl.when(e == 0)
    def _prep():
        q = _dot(hb, wq_ref[...]).astype(BF16)
        for h in range(PEER_HEADS):
            for c, ref in enumerate((s1_ref, s2_ref)):
                off = (2 * h + c) * PEER_DKEY
                ref[h] = _dot_nt(sk_ref[h, c], q[:, off:off + PEER_DKEY])

        def head(h, _):
            s1 = s1_ref[h]
            s2 = s2_ref[h]
            a = _top_rows(s1, PEER_TOPK)
            b = _top_rows(s2, PEER_TOPK)
            tau = _kth_pair_sum(a, b)
            a0, b0 = a[0:1], b[0:1]
            za = jnp.exp(a - a0)
            zb = jnp.exp(b - b0)
            zsum = jnp.zeros_like(tau)
            for i in range(PEER_TOPK):
                sel = (a[i:i + 1] + b) >= tau
                zsum = zsum + za[i:i + 1] * jnp.sum(jnp.where(sel, zb, 0.0), axis=0, keepdims=True)
            e1_ref[h] = jnp.exp(s1 - a0) * (1.0 / zsum)
            e2_ref[h] = jnp.exp(s2 - b0)
            tau_ref[h] = tau
            return 0

        lax.fori_loop(0, PEER_HEADS, head, 0)
        acc_ref[...] = jnp.zeros_like(acc_ref)
        g_ref[...] = jnp.zeros_like(g_ref)

    eb = jnp.minimum(e, nblk - 1)
    act = _dot_nt(down_ref[...], hb)
    acc_ref[...] += _dot(upt_ref[...], g_ref[(e + 1) % 2])
    for ii in range(PEER_IBLK):
        i = eb * PEER_IBLK + ii
        w = jnp.zeros((PEER_NKEYS, tm), F32)
        for h in range(PEER_HEADS):
            s1row = s1_ref[h, pl.ds(i, 1), :]
            e1row = e1_ref[h, pl.ds(i, 1), :]
            sel = (s1row + s2_ref[h]) >= tau_ref[h]
            w = w + jnp.where(sel, e2_ref[h] * e1row, 0.0)
        a = act[ii * PEER_NKEYS:(ii + 1) * PEER_NKEYS]
        gelu = 0.5 * a * (1.0 + lax.erf(a * (2.0 ** -0.5)))
        g_ref[e % 2, ii * PEER_NKEYS:(ii + 1) * PEER_NKEYS, :] = (w * gelu).astype(BF16)

    @pl.when(e == nblk)
    def _():
        o_ref[...] = x_ref[...] + acc_ref[...].T


def _peer(h2, x2d, wq, sub_keys, down, up_t, tm):
    t = x2d.shape[0]
    eblk = PEER_IBLK * PEER_NKEYS
    nblk = PEER_EXPERTS // eblk
    nheads_keys = (PEER_HEADS, PEER_NKEYS, tm)
    return pl.pallas_call(
        _peer_kernel,
        grid=(t // tm, nblk + 1),
        in_specs=[pl.BlockSpec((tm, D_MODEL), lambda i, e: (i, 0)),
                  pl.BlockSpec((tm, D_MODEL), lambda i, e: (i, 0)),
                  pl.BlockSpec(wq.shape, lambda i, e: (0, 0)),
                  pl.BlockSpec(sub_keys.shape, lambda i, e: (0, 0, 0, 0)),
                  pl.BlockSpec((eblk, D_MODEL), lambda i, e: (jnp.minimum(e, nblk - 1), 0)),
                  pl.BlockSpec((D_MODEL, eblk), lambda i, e: (0, jnp.maximum(e - 1, 0)))],
        out_specs=pl.BlockSpec((tm, D_MODEL), lambda i, e: (i, 0)),
        out_shape=jax.ShapeDtypeStruct((t, D_MODEL), F32),
        scratch_shapes=[pltpu.VMEM(nheads_keys, F32), pltpu.VMEM(nheads_keys, F32),
                        pltpu.VMEM(nheads_keys, F32), pltpu.VMEM(nheads_keys, F32),
                        pltpu.VMEM((PEER_HEADS, 1, tm), F32),
                        pltpu.VMEM((2, eblk, tm), BF16),
                        pltpu.VMEM((D_MODEL, tm), F32)],
        compiler_params=_params("arbitrary", "arbitrary"),
        name="peer",
    )(h2, x2d, wq, sub_keys, down, up_t)


def _rmsnorm_kernel(x_ref, g_ref, o_ref):
    x = x_ref[...]
    var = jnp.mean(x * x, axis=-1, keepdims=True)
    o_ref[...] = x * lax.rsqrt(var + EPS) * g_ref[...]


def _rmsnorm(x2d, g, tm):
    t, d = x2d.shape
    return pl.pallas_call(
        _rmsnorm_kernel,
        grid=(t // tm,),
        in_specs=[pl.BlockSpec((tm, d), lambda i: (i, 0)), pl.BlockSpec((1, d), lambda i: (0, 0))],
        out_specs=pl.BlockSpec((tm, d), lambda i: (i, 0)),
        out_shape=jax.ShapeDtypeStruct((t, d), F32),
        compiler_params=_params("arbitrary"),
        name="final_norm",
    )(x2d, g.reshape(1, d))


def kernel(x, norm_mix_g, w_in, conv_w, conv_b, dt_bias, a_log, d_skip, ssd_norm_g, w_branch_a, w_branch_b,
           w_branch_c, w_out, norm_ffn_g, peer_w_query, peer_sub_keys, peer_down, peer_up, final_norm_g):
    batch, seq, d = x.shape
    x2d = x.reshape(batch * seq, d)
    depth = w_in.shape[0]
    o_c = 3 * A_WIDTH
    o_zx = o_c + 3 * SB_WIDTH
    o_dt = o_zx + 3 * SSD_INNER
    o_g = o_dt + SSD_HEADS
    for l in range(depth):
        w = w_in[l]
        g = norm_mix_g[l]
        qkv_c = _norm_matmul(x2d, g, w[:, o_c:o_zx].astype(BF16), BF16, 1024, SB_WIDTH, "proj_qkv_c")
        zx = _norm_matmul(x2d, g, w[:, o_zx:o_dt].astype(BF16), F32, 1024, 1536, "proj_zx")
        dt_raw = _norm_matmul(x2d, g, w[:, o_dt:o_g].astype(BF16), F32, 1024, SSD_HEADS, "proj_dt")
        gate_logits = _norm_matmul(x2d, g, w[:, o_g:].astype(BF16), F32, 1024, 1536, "proj_gates")

        outs, lses = [], []
        for grp, (_, dilation) in enumerate(DIL_PAIRS):
            w_grp = jnp.concatenate([w[:, s * A_WIDTH + grp * A_OUT:s * A_WIDTH + (grp + 1) * A_OUT]
                                     for s in range(3)], axis=1).astype(BF16)
            qkv_g = _norm_matmul(x2d, g, w_grp, BF16, 1024, 3 * A_OUT, f"proj_qkv_a{grp}")
            o, lse = _dil_attn(qkv_g, grp, dilation, batch, seq)
            outs.append(o)
            lses.append(lse)
        y_ssd = _ssd(zx, dt_raw, conv_w[l], conv_b[l], dt_bias[l], a_log[l], d_skip[l], batch, seq)
        y_c = _sb_attn(qkv_c, batch, seq)

        x2d, h2 = _merge_out(outs, lses, y_ssd, zx, y_c, gate_logits, x2d,
                             w_branch_a[l].astype(BF16), w_branch_b[l].astype(BF16), w_branch_c[l].astype(BF16),
                             w_out[l].astype(BF16), ssd_norm_g[l], norm_ffn_g[l], 256)
        x2d = _peer(h2, x2d, peer_w_query[l].astype(BF16), peer_sub_keys[l].astype(BF16),
                    peer_down[l].astype(BF16), peer_up[l].T.astype(BF16), 256)
    return _rmsnorm(x2d, final_norm_g, 1024).reshape(batch, seq, d)
```
